```python
import math
import jax, jax.numpy as jnp
from jax import lax
import numpy as np

D_MODEL = 1024
BATCH = 4
SEQ = 8192
DEPTH = 4
DEC_BATCH = 1
DEC_SEQ = 16384
PAST_LEN = 128

N_EVEN = (DEPTH + 1) // 2
N_ODD = DEPTH // 2
EPS = 1e-6
F32 = jnp.float32

S5_WIDTH = D_MODEL // 2
S5_GROUP = 16
S5_GROUPS = S5_WIDTH // S5_GROUP
S5_STATE = 64
DT_MIN = 1e-3
DT_MAX = 1e-1

RW_WIDTH = D_MODEL // 2
RW_HEAD = 64
RW_HEADS = RW_WIDTH // RW_HEAD
RW_LORA_W = 64
RW_LORA_A = 64
RW_LN_EPS = 64e-5
RW_SHIFTED = 3 * RW_WIDTH + RW_LORA_W + RW_LORA_A

EVEN_IN = 2 * S5_WIDTH + RW_SHIFTED + RW_WIDTH
EVEN_MIX = S5_WIDTH + RW_WIDTH

AT_HEADS = 16
AT_KV_HEADS = 4
AT_GROUP = AT_HEADS // AT_KV_HEADS
AT_HEAD_DIM = D_MODEL // AT_HEADS
AT_WINDOW = 128
AT_BLOCK = 128
AT_WIDTH = AT_HEADS * AT_HEAD_DIM
AT_KV_WIDTH = AT_KV_HEADS * AT_HEAD_DIM
ODD_IN = 2 * AT_WIDTH + 2 * AT_KV_WIDTH
NEG_INF = -1e30

kernel_name = "hybrid_s5_rwkv7_swa_encoder"


def rms_norm(x, g):
    xf = x.astype(F32)
    y = xf * lax.rsqrt(jnp.mean(xf * xf, axis=-1, keepdims=True) + EPS)
    return (y * g.astype(F32)).astype(x.dtype)


def _complex_affine_op(c1, c2):
    a1r, a1i, b1r, b1i = c1
    a2r, a2i, b2r, b2i = c2
    return (a2r * a1r - a2i * a1i,
            a2r * a1i + a2i * a1r,
            a2r * b1r - a2i * b1i + b2r,
            a2r * b1i + a2i * b1r + b2i)


def s5_direction(u, a_re, a_im, log_dt, b_re, b_im, c_re, c_im, reverse):
    dt = jnp.exp(log_dt.astype(F32))[:, None]
    ar = a_re.astype(F32)
    ai = a_im.astype(F32)
    mag = jnp.exp(ar * dt)
    lr = mag * jnp.cos(ai * dt)
    li = mag * jnp.sin(ai * dt)
    den = ar * ar + ai * ai
    nr = lr - 1.0
    qr = (nr * ar + li * ai) / den
    qi = (li * ar - nr * ai) / den
    br = b_re.astype(F32)
    bi = b_im.astype(F32)
    bbr = qr[..., None] * br - qi[..., None] * bi
    bbi = qr[..., None] * bi + qi[..., None] * br
    hr = jnp.einsum('blgc,gnc->blgn', u, bbr)
    hi = jnp.einsum('blgc,gnc->blgn', u, bbi)
    shp = hr.shape
    _, _, hr, hi = lax.associative_scan(
        _complex_affine_op,
        (jnp.broadcast_to(lr, shp), jnp.broadcast_to(li, shp), hr, hi),
        reverse=reverse, axis=1)
    return (jnp.einsum('blgn,gcn->blgc', hr, c_re.astype(F32))
            - jnp.einsum('blgn,gcn->blgc', hi, c_im.astype(F32)))


def s5_branch(u, z, p):
    B, L, _ = u.shape
    uf = u.astype(F32)
    ug = uf.reshape(B, L, S5_GROUPS, S5_GROUP)
    y = (s5_direction(ug, p['s5_a_re'][0], p['s5_a_im'][0], p['s5_log_dt'][0],
                      p['s5_b_re'][0], p['s5_b_im'][0], p['s5_c_re'][0], p['s5_c_im'][0], False)
         + s5_direction(ug, p['s5_a_re'][1], p['s5_a_im'][1], p['s5_log_dt'][1],
                        p['s5_b_re'][1], p['s5_b_im'][1], p['s5_c_re'][1], p['s5_c_im'][1], True))
    y = y.reshape(B, L, S5_WIDTH) + p['s5_d'].astype(F32) * uf
    y = jax.nn.gelu(y)
    y = y * jax.nn.sigmoid(y @ p['s5_glu_w'].astype(F32) + p['s5_glu_b'].astype(F32))
    return y * jax.nn.silu(z.astype(F32))


def centred_shift(x, mu):
    xp = jnp.pad(x, ((0, 0), (1, 1), (0, 0)))
    nb = 0.5 * (xp[:, :-2] + xp[:, 2:])
    return x + mu * (nb - x)


def rwkv_scan(r, w, k, v, kk, b, reverse):
    Bsz, H, N = r.shape[1:]

    def step(S, inp):
        r_t, w_t, k_t, v_t, kk_t, b_t = inp
        sa = -jnp.einsum('bhvk,bhk->bhv', S, kk_t)
        S = (S * w_t[:, :, None, :] + sa[..., None] * b_t[:, :, None, :]
             + v_t[..., None] * k_t[:, :, None, :])
        return S, jnp.einsum('bhvk,bhk->bhv', S, r_t)

    S0 = jnp.zeros((Bsz, H, N, N), F32)
    _, y = lax.scan(step, S0, (r, w, k, v, kk, b), reverse=reverse)
    return y


def rwkv_branch(h, z, p):
    B, L, _ = h.shape
    h = centred_shift(h.astype(F32), p['rw_mu'].astype(F32))
    r, k, v, lw, la = jnp.split(h, [RW_WIDTH, 2 * RW_WIDTH, 3 * RW_WIDTH,
                                    3 * RW_WIDTH + RW_LORA_W], axis=-1)
    a = jax.nn.sigmoid(p['rw_a0'].astype(F32) + la @ p['rw_a_up'].astype(F32))
    tw = jnp.tanh(lw)

    def decay(d):
        wl = -jax.nn.softplus(-(p['rw_w0'][d].astype(F32) + tw @ p['rw_w_up'][d].astype(F32))) - 0.5
        return jnp.exp(-jnp.exp(wl))

    heads = lambda t: t.reshape(B, L, RW_HEADS, RW_HEAD)
    kk = heads(k * p['rw_k_k'].astype(F32))
    kk = kk / jnp.maximum(jnp.sqrt(jnp.sum(kk * kk, axis=-1, keepdims=True)), 1e-12)
    k = k * (1.0 + (a - 1.0) * p['rw_k_a'].astype(F32))
    rh, kh, vh, ah = heads(r), heads(k), heads(v), heads(a)
    tm = lambda t: jnp.swapaxes(t, 0, 1)
    R, K, V, KK, BB = tm(rh), tm(kh), tm(vh), tm(kk), tm(kk * ah)
    y = (rwkv_scan(R, tm(heads(decay(0))), K, V, KK, BB, False)
         + rwkv_scan(R, tm(heads(decay(1))), K, V, KK, BB, True))
    y = tm(y)
    mean = jnp.mean(y, axis=-1, keepdims=True)
    var = jnp.mean(jnp.square(y - mean), axis=-1, keepdims=True)
    y = ((y - mean) * lax.rsqrt(var + RW_LN_EPS)).reshape(B, L, RW_WIDTH)
    y = y * p['rw_ln_g'].astype(F32) + p['rw_ln_b'].astype(F32)
    bonus = jnp.sum(rh * kh * p['rw_r_k'].astype(F32), axis=-1, keepdims=True) * vh
    y = y + bonus.reshape(B, L, RW_WIDTH)
    return y * jax.nn.silu(z.astype(F32))


def alibi_slopes():
    return jnp.exp2(-8.0 * jnp.arange(1, AT_HEADS + 1, dtype=F32) / AT_HEADS)


def rms_head(t, g):
    tf = t.astype(F32)
    return tf * lax.rsqrt(jnp.mean(tf * tf, axis=-1, keepdims=True) + EPS) * g.astype(F32)


def attn_branch(h, p):
    B, L, _ = h.shape
    q, k, v, z = jnp.split(h, [AT_WIDTH, AT_WIDTH + AT_KV_WIDTH,
                               AT_WIDTH + 2 * AT_KV_WIDTH], axis=-1)
    q = rms_head(q.reshape(B, L, AT_HEADS, AT_HEAD_DIM), p['at_q_norm']) * (AT_HEAD_DIM ** -0.5)
    k = rms_head(k.reshape(B, L, AT_KV_HEADS, AT_HEAD_DIM), p['at_k_norm'])
    v = v.reshape(B, L, AT_KV_HEADS, AT_HEAD_DIM).astype(F32)
    nb = L // AT_BLOCK
    qb = q.reshape(B, nb, AT_BLOCK, AT_KV_HEADS, AT_GROUP, AT_HEAD_DIM).transpose(1, 0, 2, 3, 4, 5)

    def band(t):
        tp = jnp.pad(t, ((0, 0), (AT_BLOCK, AT_BLOCK), (0, 0), (0, 0)))
        tp = tp.reshape(B, nb + 2, AT_BLOCK, AT_KV_HEADS, AT_HEAD_DIM)
        w = jnp.concatenate([tp[:, :-2], tp[:, 1:-1], tp[:, 2:]], axis=2)
        return jnp.swapaxes(w, 0, 1)

    kb, vb = band(k), band(v)
    rel = AT_BLOCK + jnp.arange(AT_BLOCK)[:, None] - jnp.arange(3 * AT_BLOCK)[None, :]
    in_win = jnp.abs(rel) <= AT_WINDOW
    slopes = alibi_slopes().reshape(AT_KV_HEADS, AT_GROUP)
    bias = -slopes[:, :, None, None] * jnp.abs(rel).astype(F32)
    sink = p['at_sink'].astype(F32).reshape(1, AT_KV_HEADS, AT_GROUP, 1)

    def block(args):
        i, qi, ki, vi = args
        s_pos = (i - 1) * AT_BLOCK + jnp.arange(3 * AT_BLOCK)
        valid = in_win & ((s_pos >= 0) & (s_pos < L))[None, :]
        s = jnp.einsum('bqkgd,bskd->bkgqs', qi, ki) + bias
        s = jnp.where(valid, s, NEG_INF)
        m = jnp.maximum(jnp.max(s, axis=-1), sink)
        pr = jnp.exp(s - m[..., None])
        den = jnp.sum(pr, axis=-1) + jnp.exp(sink - m)
        o = jnp.einsum('bkgqs,bskd->bqkgd', pr, vi)
        return o / jnp.transpose(den, (0, 3, 1, 2))[..., None]

    o = lax.map(block, (jnp.arange(nb), qb, kb, vb))
    o = o.transpose(1, 0, 2, 3, 4, 5).reshape(B, L, AT_WIDTH)
    return o * jax.nn.silu(z.astype(F32))


def even_layer(x, p):
    h = rms_norm(x, p['norm'])
    proj = h @ p['w_in']
    u, z_s5, h_rw, z_rw = jnp.split(
        proj, [S5_WIDTH, 2 * S5_WIDTH, 2 * S5_WIDTH + RW_SHIFTED], axis=-1)
    ya = s5_branch(u, z_s5, p)
    yb = rwkv_branch(h_rw, z_rw, p)
    y = jnp.concatenate([ya, yb], axis=-1).astype(x.dtype) @ p['w_out']
    return x + y.astype(x.dtype)


def odd_layer(x, p):
    h = rms_norm(x, p['norm'])
    o = attn_branch(h @ p['w_in'], p)
    return x + (o.astype(x.dtype) @ p['w_out']).astype(x.dtype)


def setup_inputs(seed: int = 0) -> dict:
    key = jax.random.key(seed)
    ks = iter(jax.random.split(key, 40))
    nrm = lambda shape, scale: scale * jax.random.normal(next(ks), shape, F32)
    NE, NO, G, N = N_EVEN, N_ODD, S5_GROUPS, S5_STATE
    a_im_base = jnp.pi * jnp.arange(N, dtype=F32)
    return {
        'x_prompt': nrm((BATCH, SEQ, D_MODEL), 1.0),
        'x_sample': nrm((DEC_BATCH, DEC_SEQ, D_MODEL), 1.0),
        'ev_norm': 1.0 + nrm((NE, D_MODEL), 0.02),
        'ev_w_in': nrm((NE, D_MODEL, EVEN_IN), D_MODEL ** -0.5),
        's5_a_re': -0.5 + nrm((NE, 2, G, N), 0.01),
        's5_a_im': a_im_base + nrm((NE, 2, G, N), 0.01),
        's5_log_dt': jax.random.uniform(next(ks), (NE, 2, G), F32,
                                        math.log(DT_MIN), math.log(DT_MAX)),
        's5_b_re': nrm((NE, 2, G, N, S5_GROUP), (2 * S5_GROUP) ** -0.5),
        's5_b_im': nrm((NE, 2, G, N, S5_GROUP), (2 * S5_GROUP) ** -0.5),
        's5_c_re': nrm((NE, 2, G, S5_GROUP, N), (2 * N) ** -0.5),
        's5_c_im': nrm((NE, 2, G, S5_GROUP, N), (2 * N) ** -0.5),
        's5_d': nrm((NE, S5_WIDTH), 1.0),
        's5_glu_w': nrm((NE, S5_WIDTH, S5_WIDTH), S5_WIDTH ** -0.5),
        's5_glu_b': nrm((NE, S5_WIDTH), 0.01),
        'rw_mu': jax.random.uniform(next(ks), (NE, RW_SHIFTED), F32),
        'rw_w0': jax.random.uniform(next(ks), (NE, 2, RW_WIDTH), F32, -6.0, -1.0),
        'rw_w_up': nrm((NE, 2, RW_LORA_W, RW_WIDTH), 0.05),
        'rw_a0': nrm((NE, RW_WIDTH), 0.1),
        'rw_a_up': nrm((NE, RW_LORA_A, RW_WIDTH), 0.05),
        'rw_k_k': 0.85 + nrm((NE, RW_WIDTH), 0.02),
        'rw_k_a': 1.0 + nrm((NE, RW_WIDTH), 0.02),
        'rw_r_k': nrm((NE, RW_HEADS, RW_HEAD), 0.1),
        'rw_ln_g': 1.0 + nrm((NE, RW_WIDTH), 0.02),
        'rw_ln_b': nrm((NE, RW_WIDTH), 0.01),
        'ev_w_out': nrm((NE, EVEN_MIX, D_MODEL), EVEN_MIX ** -0.5),
        'od_norm': 1.0 + nrm((NO, D_MODEL), 0.02),
        'od_w_in': nrm((NO, D_MODEL, ODD_IN), D_MODEL ** -0.5),
        'at_q_norm': 1.0 + nrm((NO, AT_HEAD_DIM), 0.02),
        'at_k_norm': 1.0 + nrm((NO, AT_HEAD_DIM), 0.02),
        'at_sink': nrm((NO, AT_HEADS), 1.0),
        'od_w_out': nrm((NO, AT_WIDTH, D_MODEL), AT_WIDTH ** -0.5),
    }


def reference(x_prompt, x_sample, ev_norm, ev_w_in, s5_a_re, s5_a_im, s5_log_dt,
              s5_b_re, s5_b_im, s5_c_re, s5_c_im, s5_d, s5_glu_w, s5_glu_b,
              rw_mu, rw_w0, rw_w_up, rw_a0, rw_a_up, rw_k_k, rw_k_a, rw_r_k,
              rw_ln_g, rw_ln_b, ev_w_out, od_norm, od_w_in, at_q_norm, at_k_norm,
              at_sink, od_w_out):
    ev = {'norm': ev_norm, 'w_in': ev_w_in, 's5_a_re': s5_a_re, 's5_a_im': s5_a_im,
          's5_log_dt': s5_log_dt, 's5_b_re': s5_b_re, 's5_b_im': s5_b_im,
          's5_c_re': s5_c_re, 's5_c_im': s5_c_im, 's5_d': s5_d, 's5_glu_w': s5_glu_w,
          's5_glu_b': s5_glu_b, 'rw_mu': rw_mu, 'rw_w0': rw_w0, 'rw_w_up': rw_w_up,
          'rw_a0': rw_a0, 'rw_a_up': rw_a_up, 'rw_k_k': rw_k_k, 'rw_k_a': rw_k_a,
          'rw_r_k': rw_r_k, 'rw_ln_g': rw_ln_g, 'rw_ln_b': rw_ln_b, 'w_out': ev_w_out}
    od = {'norm': od_norm, 'w_in': od_w_in, 'at_q_norm': at_q_norm,
          'at_k_norm': at_k_norm, 'at_sink': at_sink, 'w_out': od_w_out}

    def trunk(x):
        for layer in range(DEPTH):
            j = layer // 2
            if layer % 2 == 0:
                x = even_layer(x, {n: a[j] for n, a in ev.items()})
            else:
                x = odd_layer(x, {n: a[j] for n, a in od.items()})
        return x

    y_prompt = trunk(x_prompt)
    y_sample = trunk(x_sample)
    return (y_prompt, y_sample)
```

```python
import functools
import math

import jax
import jax.numpy as jnp
from jax import lax
from jax.experimental import pallas as pl
from jax.experimental.pallas import tpu as pltpu

F32 = jnp.float32
BF16 = jnp.bfloat16

D_MODEL = 1024
EPS = 1e-6

S5_WIDTH = 512
S5_GROUP = 16
S5_GROUPS = 32
S5_STATE = 64
S5_LANES = S5_GROUPS * S5_STATE
S5_SUB = 16
S5_MID = 7.5
S5_TILE = 256

RW_WIDTH = 512
RW_HEAD = 64
RW_SHIFTED = 3 * RW_WIDTH + 128
RW_LN_EPS = 64e-5
RW_CHUNK = 64
RW_TILE = 256
RW_GROUP = 256

AT_HEADS = 16
AT_KV_HEADS = 4
AT_GROUP = 4
AT_HEAD_DIM = 64
AT_BLOCK = 128
AT_WIDTH = 1024
AT_KV_WIDTH = 256
NEG_INF = -1e30

TOK_TILE = 512
VMEM_LIMIT = 48 * 1024 * 1024


def _dot(a, b):
    return jnp.dot(a, b, preferred_element_type=F32)


def _dot_nt(a, b):
    return lax.dot_general(a, b, (((1,), (1,)), ((), ())), preferred_element_type=F32)


def _dot_tn(a, b):
    return lax.dot_general(a, b, (((0,), (0,)), ((), ())), preferred_element_type=F32)


def _split3(x):
    x1 = x.astype(BF16)
    r1 = x - x1.astype(F32)
    x2 = r1.astype(BF16)
    x3 = (r1 - x2.astype(F32)).astype(BF16)
    return x1, x2, x3


def _split2(x):
    x1 = x.astype(BF16)
    x2 = (x - x1.astype(F32)).astype(BF16)
    return x1, x2


def _sigmoid(x):
    return 1.0 / (1.0 + jnp.exp(-x))


def _silu(x):
    return x * _sigmoid(x)


def _proj_in_kernel(x_ref, g_ref, w_ref, *out_refs, splits):
    x = x_ref[...]
    ms = jnp.mean(x * x, axis=-1, keepdims=True)
    h = (x * lax.rsqrt(ms + EPS) * g_ref[...]).astype(BF16)
    off = 0
    for o_ref, n in zip(out_refs, splits):
        o_ref[...] = _dot(h, w_ref[:, off:off + n])
        off += n


def proj_in(x2d, g, w_bf16, splits):
    m = x2d.shape[0]
    n = w_bf16.shape[1]
    assert sum(splits) == n and m % TOK_TILE == 0
    return pl.pallas_call(
        functools.partial(_proj_in_kernel, splits=splits),
        out_shape=[jax.ShapeDtypeStruct((m, s), F32) for s in splits],
        grid=(m // TOK_TILE,),
        in_specs=[pl.BlockSpec((TOK_TILE, D_MODEL), lambda i: (i, 0)),
                  pl.BlockSpec((1, D_MODEL), lambda i: (0, 0)),
                  pl.BlockSpec((D_MODEL, n), lambda i: (0, 0))],
        out_specs=[pl.BlockSpec((TOK_TILE, s), lambda i: (i, 0)) for s in splits],
        compiler_params=pltpu.CompilerParams(dimension_semantics=("parallel",),
                                             vmem_limit_bytes=VMEM_LIMIT),
        name="proj_in",
    )(x2d, g.reshape(1, D_MODEL), w_bf16)


def _proj_out_kernel(*refs, n_in):
    x_ref = refs[0]
    y_refs = refs[1:1 + n_in]
    w_refs = refs[1 + n_in:1 + 2 * n_in]
    o_ref = refs[1 + 2 * n_in]
    acc = x_ref[...]
    for y_ref, w_ref in zip(y_refs, w_refs):
        acc = acc + _dot(y_ref[...].astype(BF16), w_ref[...])
    o_ref[...] = acc


def proj_out(x2d, ys, ws_bf16):
    m = x2d.shape[0]
    n_in = len(ys)
    in_specs = [pl.BlockSpec((TOK_TILE, D_MODEL), lambda i: (i, 0))]
    in_specs += [pl.BlockSpec((TOK_TILE, y.shape[1]), lambda i: (i, 0)) for y in ys]
    in_specs += [pl.BlockSpec(w.shape, lambda i: (0, 0)) for w in ws_bf16]
    return pl.pallas_call(
        functools.partial(_proj_out_kernel, n_in=n_in),
        out_shape=jax.ShapeDtypeStruct((m, D_MODEL), F32),
        grid=(m // TOK_TILE,),
        in_specs=in_specs,
        out_specs=pl.BlockSpec((TOK_TILE, D_MODEL), lambda i: (i, 0)),
        compiler_params=pltpu.CompilerParams(dimension_semantics=("parallel",),
                                             vmem_limit_bytes=VMEM_LIMIT),
        name="proj_out",
    )(x2d, *ys, *ws_bf16)


def _s5_kernel(*refs, reverse, final):
    if final:
        (u_ref, wb_ref, wc_ref, ein_r, ein_i, eout_r, eout_i, cin_r, cin_i,
         yprev_ref, z_ref, d_ref, gw_ref, gb_ref,
         o_ref, hin_r, hin_i, h_r, h_i, car_r, car_i) = refs
    else:
        (u_ref, wb_ref, wc_ref, ein_r, ein_i, eout_r, eout_i, cin_r, cin_i,
         o_ref, hin_r, hin_i, h_r, h_i, car_r, car_i) = refs

    @pl.when(pl.program_id(1) == 0)
    def _():
        car_r[...] = jnp.zeros_like(car_r)
        car_i[...] = jnp.zeros_like(car_i)

    u = u_ref[0]
    ub16 = u.astype(BF16)
    for ub in range(4):
        bu = _dot(ub16[:, ub * 128:(ub + 1) * 128], wb_ref[ub])
        hin_r[:, ub * 512:(ub + 1) * 512] = bu[:, :512]
        hin_i[:, ub * 512:(ub + 1) * 512] = bu[:, 512:]

    row = lax.broadcasted_iota(jnp.int32, (S5_SUB, S5_SUB), 0)
    col = lax.broadcasted_iota(jnp.int32, (S5_SUB, S5_SUB), 1)
    tri = jnp.where((col >= row) if reverse else (col <= row), 1.0, 0.0).astype(BF16)
    n_sub = S5_TILE // S5_SUB
    last = 0 if reverse else S5_SUB - 1

    def body(jj, carry):
        j = (n_sub - 1 - jj) if reverse else jj
        rows = pl.ds(pl.multiple_of(j * S5_SUB, S5_SUB), S5_SUB)
        xr = hin_r[rows, :]
        xi = hin_i[rows, :]
        er = ein_r[...]
        ei = ein_i[...]
        sr = (xr * er - xi * ei).astype(BF16)
        si = (xr * ei + xi * er).astype(BF16)
        cr = car_r[...]
        ci = car_i[...]
        add_r = cin_r[...] * cr - cin_i[...] * ci
        add_i = cin_r[...] * ci + cin_i[...] * cr
        cum_r = _dot(tri, sr) + add_r
        cum_i = _dot(tri, si) + add_i
        fr = eout_r[...]
        fi = eout_i[...]
        hr = cum_r * fr - cum_i * fi
        hi = cum_r * fi + cum_i * fr
        h_r[rows, :] = hr
        h_i[rows, :] = hi
        car_r[...] = hr[last:last + 1, :]
        car_i[...] = hi[last:last + 1, :]
        return carry

    lax.fori_loop(0, n_sub, body, 0)

    ys = []
    for ub in range(4):
        hcat = jnp.concatenate([h_r[:, ub * 512:(ub + 1) * 512],
                                h_i[:, ub * 512:(ub + 1) * 512]], axis=1).astype(BF16)
        ys.append(_dot(hcat, wc_ref[ub]))
    y = jnp.concatenate(ys, axis=1)
    if not final:
        o_ref[0] = y
        return
    y = y + yprev_ref[0] + d_ref[...] * u
    y = jax.nn.gelu(y)
    gate = _dot(y.astype(BF16), gw_ref[...]) + gb_ref[...]
    y = y * _sigmoid(gate)
    o_ref[0] = y * _silu(z_ref[0])


def s5_sweep(u, tabs, reverse, final_args=None):
    b, l, _ = u.shape
    nt = l // S5_TILE
    final = final_args is not None
    if reverse:
        tok = lambda bi, i: (bi, nt - 1 - i, 0)
    else:
        tok = lambda bi, i: (bi, i, 0)
    c2 = lambda bi, i: (0, 0)
    c3 = lambda bi, i: (0, 0, 0)
    wb, wc, ein_r, ein_i, eout_r, eout_i, cin_r, cin_i = tabs
    in_specs = [pl.BlockSpec((1, S5_TILE, S5_WIDTH), tok),
                pl.BlockSpec(wb.shape, c3), pl.BlockSpec(wc.shape, c3)]
    in_specs += [pl.BlockSpec(t.shape, c2) for t in (ein_r, ein_i, eout_r, eout_i, cin_r, cin_i)]
    args = [u, wb, wc, ein_r, ein_i, eout_r, eout_i, cin_r, cin_i]
    if final:
        yprev, z, d, gw, gb = final_args
        in_specs += [pl.BlockSpec((1, S5_TILE, S5_WIDTH), tok),
                     pl.BlockSpec((1, S5_TILE, S5_WIDTH), tok),
                     pl.BlockSpec(d.shape, c2), pl.BlockSpec(gw.shape, c2),
                     pl.BlockSpec(gb.shape, c2)]
        args += [yprev, z, d, gw, gb]
    lanes = S5_LANES
    return pl.pallas_call(
        functools.partial(_s5_kernel, reverse=reverse, final=final),
        out_shape=jax.ShapeDtypeStruct((b, l, S5_WIDTH), F32),
        grid=(b, nt),
        in_specs=in_specs,
        out_specs=pl.BlockSpec((1, S5_TILE, S5_WIDTH), tok),
        scratch_shapes=[pltpu.VMEM((S5_TILE, lanes), F32), pltpu.VMEM((S5_TILE, lanes), F32),
                        pltpu.VMEM((S5_TILE, lanes), F32), pltpu.VMEM((S5_TILE, lanes), F32),
                        pltpu.VMEM((1, lanes), F32), pltpu.VMEM((1, lanes), F32)],
        compiler_params=pltpu.CompilerParams(dimension_semantics=("arbitrary", "arbitrary"),
                                             vmem_limit_bytes=VMEM_LIMIT),
        name="s5_final" if final else "s5_sweep",
    )(*args)


def _s5_tables(a_re, a_im, log_dt, b_re, b_im, c_re, c_im, reverse):
    g, n = S5_GROUPS, S5_STATE
    dt = jnp.exp(log_dt.astype(F32))[:, None]
    ar = a_re.astype(F32)
    ai = a_im.astype(F32)
    mag = jnp.exp(ar * dt)
    lr = mag * jnp.cos(ai * dt)
    li = mag * jnp.sin(ai * dt)
    den = ar * ar + ai * ai
    nr = lr - 1.0
    qr = (nr * ar + li * ai) / den
    qi = (li * ar - nr * ai) / den
    br = b_re.astype(F32)
    bi = b_im.astype(F32)
    bbr = qr[..., None] * br - qi[..., None] * bi
    bbi = qr[..., None] * bi + qi[..., None] * br
    eye = jnp.eye(8, dtype=F32)

    def in_blocks(t):
        t = t.transpose(0, 2, 1).reshape(4, 8, S5_GROUP, n)
        return jnp.einsum('ugcn,gh->ugchn', t, eye).reshape(4, 128, 512)

    wb = jnp.concatenate([in_blocks(bbr), in_blocks(bbi)], axis=2).astype(BF16)

    def out_blocks(t):
        t = t.astype(F32).transpose(0, 2, 1).reshape(4, 8, n, S5_GROUP)
        return jnp.einsum('ugnc,gh->ugnhc', t, eye).reshape(4, 512, 128)

    wc = jnp.concatenate([out_blocks(c_re), -out_blocks(c_im)], axis=1).astype(BF16)

    la = (ar * dt).reshape(1, -1)
    th = (ai * dt).reshape(1, -1)
    pos = jnp.arange(S5_SUB, dtype=F32)[:, None]

    def powers(e):
        m = jnp.exp(e * la)
        return m * jnp.cos(e * th), m * jnp.sin(e * th)

    if reverse:
        ein = powers(pos - S5_MID)
        eout = powers(S5_MID - pos)
        cin = powers(jnp.full((1, 1), S5_SUB - S5_MID, F32))
    else:
        ein = powers(S5_MID - pos)
        eout = powers(pos - S5_MID)
        cin = powers(jnp.full((1, 1), S5_MID + 1.0, F32))
    return (wb, wc, ein[0], ein[1], eout[0], eout[1], cin[0], cin[1])


def _rw_prep_kernel(x_ref, xp_ref, xn_ref, mu_ref, aup_ref, wup0_ref, wup1_ref,
                    a0_ref, w00_ref, w01_ref, kk_ref, ka_ref, ones_ref,
                    r_o, k_o, v_o, kk_o, b_o, l0_o, l1_o, *, tile):
    i = pl.program_id(1)
    nt = pl.num_programs(1)
    x = x_ref[0]
    rowi = lax.broadcasted_iota(jnp.int32, x.shape, 0)
    prev_row = xp_ref[0][7:8, :] * jnp.where(i > 0, 1.0, 0.0)
    next_row = xn_ref[0][0:1, :] * jnp.where(i < nt - 1, 1.0, 0.0)
    x_prev = jnp.where(rowi == 0, prev_row, pltpu.roll(x, 1, 0))
    x_next = jnp.where(rowi == tile - 1, next_row, pltpu.roll(x, tile - 1, 0))
    nb = 0.5 * (x_prev + x_next)
    xs = x + mu_ref[...] * (nb - x)
    r = xs[:, 0:512]
    k = xs[:, 512:1024]
    v = xs[:, 1024:1536]
    lt = xs[:, 1536:1664]
    a = _sigmoid(a0_ref[...] + _dot(lt.astype(BF16), aup_ref[...]))
    tw = jnp.tanh(lt).astype(BF16)
    c = math.exp(-0.5)
    l0 = -c * _sigmoid(w00_ref[...] + _dot(tw, wup0_ref[...]))
    l1 = -c * _sigmoid(w01_ref[...] + _dot(tw, wup1_ref[...]))
    kk = k * kk_ref[...]
    s1, s2 = _split2(kk * kk)
    ss = _dot(s1, ones_ref[...]) + _dot(s2, ones_ref[...])
    kk = kk / jnp.maximum(jnp.sqrt(ss), 1e-12)
    r_o[0] = r
    k_o[0] = k * (1.0 + (a - 1.0) * ka_ref[...])
    v_o[0] = v
    kk_o[0] = kk
    b_o[0] = kk * a
    l0_o[0] = l0
    l1_o[0] = l1


def rw_prep(h_rw, p, tile=256):
    b, l, w = h_rw.shape
    nt = l // tile
    tb = tile // 8
    tok = lambda bi, i: (bi, i, 0)
    c2 = lambda bi, i: (0, 0)
    params = [p['mu'], p['aup'], p['wup0'], p['wup1'], p['a0'], p['w00'], p['w01'],
              p['kk'], p['ka'], p['ones']]
    in_specs = [pl.BlockSpec((1, tile, w), tok),
                pl.BlockSpec((1, 8, w), lambda bi, i: (bi, jnp.maximum(i * tb - 1, 0), 0)),
                pl.BlockSpec((1, 8, w), lambda bi, i: (bi, jnp.minimum((i + 1) * tb, nt * tb - 1), 0))]
    in_specs += [pl.BlockSpec(a.shape, c2) for a in params]
    return pl.pallas_call(
        functools.partial(_rw_prep_kernel, tile=tile),
        out_shape=[jax.ShapeDtypeStruct((b, l, RW_WIDTH), F32)] * 7,
        grid=(b, nt),
        in_specs=in_specs,
        out_specs=[pl.BlockSpec((1, tile, RW_WIDTH), tok)] * 7,
        compiler_params=pltpu.CompilerParams(dimension_semantics=("parallel", "parallel"),
                                             vmem_limit_bytes=VMEM_LIMIT),
        name="rw_prep",
    )(h_rw, h_rw, h_rw, *params)


def _rw_chunk(r, k, v, kk, b, lw, s_ref, masks, reverse):
    tri, strict, incl, bmask, eye = masks
    c = RW_CHUNK

    def bd(x):
        xb = x.astype(BF16)
        return jnp.where(bmask, jnp.concatenate([xb, xb, xb, xb], axis=0), jnp.zeros((), BF16))

    l1, l2, l3 = _split3(lw)
    cl = _dot(tri, l1) + _dot(tri, l2) + _dot(tri, l3)
    cl_end = cl[0:1, :] if reverse else cl[c - 1:c, :]
    e_in = jnp.exp(cl)
    e_neg = jnp.exp(-cl)
    e_end = jnp.exp(cl_end - cl)
    kkt = kk * jnp.exp(cl - lw)
    rt = r * e_in
    lhs = jnp.concatenate([kkt, rt], axis=0).astype(BF16)
    ab = _dot_nt(lhs, bd(b * e_neg))
    ak = _dot_nt(lhs, bd(k * e_neg))
    a_bb = jnp.where(strict, ab[:c], 0.0)
    a_rb = jnp.where(incl, ab[c:], 0.0)
    a_bk = jnp.where(strict, ak[:c], 0.0)
    a_rk = jnp.where(incl, ak[c:], 0.0)
    bdv = bd(v)

    pw = -a_bb
    t = eye + pw
    pw = _dot(pw.astype(BF16), bd(pw))
    for _ in range(4):
        res = _dot(jnp.concatenate([pw, t], axis=0).astype(BF16), bd(pw))
        pw = res[:c]
        t = t + res[c:]
    inv = t + _dot(t.astype(BF16), bd(pw))

    s0 = s_ref[...]
    sh = _dot_nt(lhs, s0.astype(BF16))
    rhs = sh[:c] + _dot(a_bk.astype(BF16), bdv)
    u = -_dot(inv.astype(BF16), bd(rhs))
    y = sh[c:] + _dot(jnp.concatenate([a_rb, a_rk], axis=1).astype(BF16),
                      jnp.concatenate([bd(u), bdv], axis=0))
    upd = _dot_tn(jnp.concatenate([u, v], axis=0).astype(BF16),
                  jnp.concatenate([b * e_end, k * e_end], axis=0).astype(BF16))
    s_ref[...] = jnp.where(bmask, s0 * jnp.exp(cl_end) + upd, 0.0)
    return y


def _rw_masks(reverse):
    c, g = RW_CHUNK, RW_GROUP
    r2 = lax.broadcasted_iota(jnp.int32, (c, c), 0)
    c2 = lax.broadcasted_iota(jnp.int32, (c, c), 1)
    tri = jnp.where((c2 >= r2) if reverse else (c2 <= r2), 1.0, 0.0).astype(BF16)
    t = lax.broadcasted_iota(jnp.int32, (c, g), 0)
    s = lax.broadcasted_iota(jnp.int32, (c, g), 1) & (c - 1)
    strict = (s > t) if reverse else (s < t)
    incl = (s >= t) if reverse else (s <= t)
    eye = jnp.where(s == t, 1.0, 0.0).astype(F32)
    bi = lax.broadcasted_iota(jnp.int32, (g, g), 0) >> 6
    bj = lax.broadcasted_iota(jnp.int32, (g, g), 1) >> 6
    return tri, strict, incl, bi == bj, eye


def _rw_scan_kernel(*refs, reverse, final):
    if final:
        (r_ref, k_ref, v_ref, kk_ref, b_ref, lw_ref, yprev_ref, z_ref,
         rk_ref, lng_ref, lnb_ref, ones_ref, o_ref, s_ref) = refs
    else:
        (r_ref, k_ref, v_ref, kk_ref, b_ref, lw_ref, o_ref, s_ref) = refs

    @pl.when(pl.program_id(1) == 0)
    def _():
        s_ref[...] = jnp.zeros_like(s_ref)

    masks = _rw_masks(reverse)
    n_chunks = RW_TILE // RW_CHUNK
    order = range(n_chunks - 1, -1, -1) if reverse else range(n_chunks)
    for ci in order:
        rows = slice(ci * RW_CHUNK, (ci + 1) * RW_CHUNK)
        for g in range(RW_WIDTH // RW_GROUP):
            lanes = slice(g * RW_GROUP, (g + 1) * RW_GROUP)
            y = _rw_chunk(r_ref[0, rows, lanes], k_ref[0, rows, lanes], v_ref[0, rows, lanes],
                          kk_ref[0, rows, lanes], b_ref[0, rows, lanes], lw_ref[0, rows, lanes],
                          s_ref.at[g], masks, reverse)
            o_ref[0, rows, lanes] = y

    if not final:
        return
    ones = ones_ref[...]
    inv_n = 1.0 / RW_HEAD

    def head_sum(x):
        x1, x2 = _split2(x)
        return _dot(x1, ones) + _dot(x2, ones)

    y = o_ref[0] + yprev_ref[0]
    mean = head_sum(y) * inv_n
    d = y - mean
    var = head_sum(d * d) * inv_n
    yn = d * lax.rsqrt(var + RW_LN_EPS) * lng_ref[...] + lnb_ref[...]
    bonus = head_sum(r_ref[0] * k_ref[0] * rk_ref[...]) * v_ref[0]
    o_ref[0] = (yn + bonus) * _silu(z_ref[0])


def rw_sweep(seqs, reverse, final_args=None):
    b, l, _ = seqs[0].shape
    nt = l // RW_TILE
    final = final_args is not None
    if reverse:
        tok = lambda bi, i: (bi, nt - 1 - i, 0)
    else:
        tok = lambda bi, i: (bi, i, 0)
    c2 = lambda bi, i: (0, 0)
    tok_spec = pl.BlockSpec((1, RW_TILE, RW_WIDTH), tok)
    in_specs = [tok_spec] * 6
    args = list(seqs)
    if final:
        yprev, z, rk, lng, lnb, ones = final_args
        in_specs += [tok_spec, tok_spec]
        in_specs += [pl.BlockSpec(a.shape, c2) for a in (rk, lng, lnb, ones)]
        args += [yprev, z, rk, lng, lnb, ones]
    return pl.pallas_call(
        functools.partial(_rw_scan_kernel, reverse=reverse, final=final),
        out_shape=jax.ShapeDtypeStruct((b, l, RW_WIDTH), F32),
        grid=(b, nt),
        in_specs=in_specs,
        out_specs=tok_spec,
        scratch_shapes=[pltpu.VMEM((RW_WIDTH // RW_GROUP, RW_GROUP, RW_GROUP), F32)],
        compiler_params=pltpu.CompilerParams(dimension_semantics=("arbitrary", "arbitrary"),
                                             vmem_limit_bytes=VMEM_LIMIT),
        name="rw_final" if final else "rw_sweep",
    )(*args)


def _attn_kernel(sink_ref, q_ref, kp_ref, kc_ref, kn_ref, vp_ref, vc_ref, vn_ref, z_ref,
                 qn_ref, kn_g_ref, o_ref):
    i = pl.program_id(1)
    nb = pl.num_programs(1)
    blk = AT_BLOCK
    iq = lax.broadcasted_iota(jnp.int32, (blk, 3 * blk), 0)
    jk = lax.broadcasted_iota(jnp.int32, (blk, 3 * blk), 1)
    rel = blk + iq - jk
    absrel = jnp.abs(rel)
    key_lo = jnp.where(i > 0, 0, blk)
    key_hi = jnp.where(i < nb - 1, 3 * blk, 2 * blk)
    valid = (absrel <= blk) & (jk >= key_lo) & (jk < key_hi)
    absrel_f = absrel.astype(F32)
    qg = qn_ref[...]
    kg = kn_g_ref[...]
    scale = AT_HEAD_DIM ** -0.5

    def rms(t, g):
        return t * lax.rsqrt(jnp.mean(t * t, axis=-1, keepdims=True) + EPS) * g

    q_all = q_ref[0]
    outs = []
    for kvh in range(AT_KV_HEADS):
        ks = slice(kvh * AT_HEAD_DIM, (kvh + 1) * AT_HEAD_DIM)
        kband = jnp.concatenate([kp_ref[0][:, ks], kc_ref[0][:, ks], kn_ref[0][:, ks]], axis=0)
        vband = jnp.concatenate([vp_ref[0][:, ks], vc_ref[0][:, ks], vn_ref[0][:, ks]], axis=0)
        kband = rms(kband, kg).astype(BF16)
        vband = vband.astype(BF16)
        for g in range(AT_GROUP):
            h = kvh * AT_GROUP + g
            qh = rms(q_all[:, h * AT_HEAD_DIM:(h + 1) * AT_HEAD_DIM], qg) * scale
            s = _dot_nt(qh.astype(BF16), kband)
            slope = 2.0 ** (-8.0 * (h + 1) / AT_HEADS)
            s = jnp.where(valid, s - slope * absrel_f, NEG_INF)
            sink = sink_ref[h]
            m = jnp.maximum(jnp.max(s, axis=-1, keepdims=True), sink)
            p = jnp.exp(s - m)
            den = jnp.sum(p, axis=-1, keepdims=True) + jnp.exp(sink - m)
            outs.append(_dot(p.astype(BF16), vband) / den)
    o = jnp.concatenate(outs, axis=1)
    o_ref[0] = o * _silu(z_ref[0])


def attention(q, k, v, z, sink, qn, kn):
    b, l, _ = q.shape
    nb = l // AT_BLOCK
    cur = lambda bi, i, s: (bi, i, 0)
    prv = lambda bi, i, s: (bi, jnp.maximum(i - 1, 0), 0)
    nxt = lambda bi, i, s: (bi, jnp.minimum(i + 1, nb - 1), 0)
    c2 = lambda bi, i, s: (0, 0)
    kv = lambda f: pl.BlockSpec((1, AT_BLOCK, AT_KV_WIDTH), f)
    wide = pl.BlockSpec((1, AT_BLOCK, AT_WIDTH), cur)
    grid_spec = pltpu.PrefetchScalarGridSpec(
        num_scalar_prefetch=1,
        grid=(b, nb),
        in_specs=[wide, kv(prv), kv(cur), kv(nxt), kv(prv), kv(cur), kv(nxt), wide,
                  pl.BlockSpec((1, AT_HEAD_DIM), c2), pl.BlockSpec((1, AT_HEAD_DIM), c2)],
        out_specs=wide,
    )
    return pl.pallas_call(
        _attn_kernel,
        out_shape=jax.ShapeDtypeStruct((b, l, AT_WIDTH), F32),
        grid_spec=grid_spec,
        compiler_params=pltpu.CompilerParams(dimension_semantics=("parallel", "parallel"),
                                             vmem_limit_bytes=VMEM_LIMIT),
        name="attn",
    )(sink, q, k, k, k, v, v, v, z, qn.reshape(1, -1), kn.reshape(1, -1))


def _block_ones(width, block):
    i = jnp.arange(width) // block
    return (i[:, None] == i[None, :]).astype(BF16)


def _even_params(j, ev):
    g = lambda name: ev[name][j]
    row = lambda a: a.astype(F32).reshape(1, -1)
    p = {}
    p['norm'] = g('norm')
    p['w_in'] = g('w_in').astype(BF16)
    p['w_out_a'] = g('w_out')[:S5_WIDTH].astype(BF16)
    p['w_out_b'] = g('w_out')[S5_WIDTH:].astype(BF16)
    p['s5'] = [
        _s5_tables(g('s5_a_re')[d], g('s5_a_im')[d], g('s5_log_dt')[d], g('s5_b_re')[d],
                   g('s5_b_im')[d], g('s5_c_re')[d], g('s5_c_im')[d], reverse=bool(d))
        for d in range(2)]
    p['s5_d'] = row(g('s5_d'))
    p['glu_w'] = g('s5_glu_w').astype(BF16)
    p['glu_b'] = row(g('s5_glu_b'))
    zeros64 = jnp.zeros((64, RW_WIDTH), F32)
    p['mu'] = row(g('rw_mu'))
    p['aup'] = jnp.concatenate([zeros64, g('rw_a_up').astype(F32)], axis=0).astype(BF16)
    p['wup0'] = jnp.concatenate([g('rw_w_up')[0].astype(F32), zeros64], axis=0).astype(BF16)
    p['wup1'] = jnp.concatenate([g('rw_w_up')[1].astype(F32), zeros64], axis=0).astype(BF16)
    p['a0'] = row(g('rw_a0'))
    p['w00'] = row(g('rw_w0')[0])
    p['w01'] = row(g('rw_w0')[1])
    p['kk'] = row(g('rw_k_k'))
    p['ka'] = row(g('rw_k_a'))
    p['ones'] = _block_ones(RW_WIDTH, RW_HEAD)
    p['rk'] = row(g('rw_r_k'))
    p['lng'] = row(g('rw_ln_g'))
    p['lnb'] = row(g('rw_ln_b'))
    return p


def _even_layer(x, p):
    b, l, _ = x.shape
    x2 = x.reshape(b * l, D_MODEL)
    u, z_s5, h_rw, z_rw = proj_in(x2, p['norm'], p['w_in'],
                                  (S5_WIDTH, S5_WIDTH, RW_SHIFTED, RW_WIDTH))
    u = u.reshape(b, l, -1)
    z_s5 = z_s5.reshape(b, l, -1)
    h_rw = h_rw.reshape(b, l, -1)
    z_rw = z_rw.reshape(b, l, -1)

    y_rev = s5_sweep(u, p['s5'][1], reverse=True)
    ya = s5_sweep(u, p['s5'][0], reverse=False,
                  final_args=(y_rev, z_s5, p['s5_d'], p['glu_w'], p['glu_b']))

    r, k, v, kk, bb, l0, l1 = rw_prep(h_rw, p)
    yb_rev = rw_sweep((r, k, v, kk, bb, l1), reverse=True)
    yb = rw_sweep((r, k, v, kk, bb, l0), reverse=False,
                  final_args=(yb_rev, z_rw, p['rk'], p['lng'], p['lnb'], p['ones']))

    out = proj_out(x2, [ya.reshape(b * l, -1), yb.reshape(b * l, -1)],
                   [p['w_out_a'], p['w_out_b']])
    return out.reshape(b, l, D_MODEL)


def _odd_layer(x, p):
    b, l, _ = x.shape
    x2 = x.reshape(b * l, D_MODEL)
    q, k, v, z = proj_in(x2, p['norm'], p['w_in'], (AT_WIDTH, AT_KV_WIDTH, AT_KV_WIDTH, AT_WIDTH))
    sh = lambda t: t.reshape(b, l, -1)
    o = attention(sh(q), sh(k), sh(v), sh(z), p['sink'], p['qn'], p['kn'])
    out = proj_out(x2, [o.reshape(b * l, -1)], [p['w_out']])
    return out.reshape(b, l, D_MODEL)


def kernel(x_prompt, x_sample, ev_norm, ev_w_in, s5_a_re, s5_a_im, s5_log_dt, s5_b_re, s5_b_im, s5_c_re, s5_c_im, s5_d, s5_glu_w, s5_glu_b, rw_mu, rw_w0, rw_w_up, rw_a0, rw_a_up, rw_k_k, rw_k_a, rw_r_k, rw_ln_g, rw_ln_b, ev_w_out, od_norm, od_w_in, at_q_norm, at_k_norm, at_sink, od_w_out):
    ev = {'norm': ev_norm, 'w_in': ev_w_in, 's5_a_re': s5_a_re, 's5_a_im': s5_a_im,
          's5_log_dt': s5_log_dt, 's5_b_re': s5_b_re, 's5_b_im': s5_b_im,
          's5_c_re': s5_c_re, 's5_c_im': s5_c_im, 's5_d': s5_d, 's5_glu_w': s5_glu_w,
          's5_glu_b': s5_glu_b, 'rw_mu': rw_mu, 'rw_w0': rw_w0, 'rw_w_up': rw_w_up,
          'rw_a0': rw_a0, 'rw_a_up': rw_a_up, 'rw_k_k': rw_k_k, 'rw_k_a': rw_k_a,
          'rw_r_k': rw_r_k, 'rw_ln_g': rw_ln_g, 'rw_ln_b': rw_ln_b, 'w_out': ev_w_out}
    n_even = ev_norm.shape[0]
    n_odd = od_norm.shape[0]
    even = [_even_params(j, ev) for j in range(n_even)]
    odd = [{'norm': od_norm[j], 'w_in': od_w_in[j].astype(BF16),
            'qn': at_q_norm[j].astype(F32), 'kn': at_k_norm[j].astype(F32),
            'sink': at_sink[j].astype(F32), 'w_out': od_w_out[j].astype(BF16)}
           for j in range(n_odd)]

    def trunk(x):
        for layer in range(n_even + n_odd):
            j = layer // 2
            x = _even_layer(x, even[j]) if layer % 2 == 0 else _odd_layer(x, odd[j])
        return x

    return (trunk(x_prompt), trunk(x_sample))
```

```python
import functools
import math

import jax
import jax.numpy as jnp
from jax import lax
from jax.experimental import pallas as pl
from jax.experimental.pallas import tpu as pltpu

F32 = jnp.float32
BF16 = jnp.bfloat16

D_MODEL = 1024
EPS = 1e-6

S5_WIDTH = 512
S5_GROUP = 16
S5_GROUPS = 32
S5_STATE = 64
S5_LANES = S5_GROUPS * S5_STATE
S5_SUB = 16
S5_MID = 7.5
S5_TILE = 256

RW_WIDTH = 512
RW_HEAD = 64
RW_SHIFTED = 3 * RW_WIDTH + 128
RW_LN_EPS = 64e-5
RW_CHUNK = 64
RW_TILE = 512
RW_WAVE = 4
RW_GROUP = 256

AT_HEADS = 16
AT_KV_HEADS = 4
AT_GROUP = 4
AT_HEAD_DIM = 64
AT_BLOCK = 128
AT_WIDTH = 1024
AT_KV_WIDTH = 256
NEG_INF = -1e30

TOK_TILE = 512
VMEM_LIMIT = 48 * 1024 * 1024


def _dot(a, b):
    return jnp.dot(a, b, preferred_element_type=F32)


def _dot_nt(a, b):
    return lax.dot_general(a, b, (((1,), (1,)), ((), ())), preferred_element_type=F32)


def _dot_tn(a, b):
    return lax.dot_general(a, b, (((0,), (0,)), ((), ())), preferred_element_type=F32)


def _split3(x):
    x1 = x.astype(BF16)
    r1 = x - x1.astype(F32)
    x2 = r1.astype(BF16)
    x3 = (r1 - x2.astype(F32)).astype(BF16)
    return x1, x2, x3


def _split2(x):
    x1 = x.astype(BF16)
    x2 = (x - x1.astype(F32)).astype(BF16)
    return x1, x2


def _sigmoid(x):
    return 1.0 / (1.0 + jnp.exp(-x))


def _silu(x):
    return x * _sigmoid(x)


def _proj_in_kernel(x_ref, g_ref, w_ref, *out_refs, splits):
    x = x_ref[...]
    ms = jnp.mean(x * x, axis=-1, keepdims=True)
    h = (x * lax.rsqrt(ms + EPS) * g_ref[...]).astype(BF16)
    off = 0
    for o_ref, n in zip(out_refs, splits):
        o_ref[...] = _dot(h, w_ref[:, off:off + n])
        off += n


def proj_in(x2d, g, w_bf16, splits):
    m = x2d.shape[0]
    n = w_bf16.shape[1]
    assert sum(splits) == n and m % TOK_TILE == 0
    return pl.pallas_call(
        functools.partial(_proj_in_kernel, splits=splits),
        out_shape=[jax.ShapeDtypeStruct((m, s), F32) for s in splits],
        grid=(m // TOK_TILE,),
        in_specs=[pl.BlockSpec((TOK_TILE, D_MODEL), lambda i: (i, 0)),
                  pl.BlockSpec((1, D_MODEL), lambda i: (0, 0)),
                  pl.BlockSpec((D_MODEL, n), lambda i: (0, 0))],
        out_specs=[pl.BlockSpec((TOK_TILE, s), lambda i: (i, 0)) for s in splits],
        compiler_params=pltpu.CompilerParams(dimension_semantics=("parallel",),
                                             vmem_limit_bytes=VMEM_LIMIT),
        name="proj_in",
    )(x2d, g.reshape(1, D_MODEL), w_bf16)


def _proj_out_kernel(*refs, n_in):
    x_ref = refs[0]
    y_refs = refs[1:1 + n_in]
    w_refs = refs[1 + n_in:1 + 2 * n_in]
    o_ref = refs[1 + 2 * n_in]
    acc = x_ref[...]
    for y_ref, w_ref in zip(y_refs, w_refs):
        acc = acc + _dot(y_ref[...].astype(BF16), w_ref[...])
    o_ref[...] = acc


def proj_out(x2d, ys, ws_bf16):
    m = x2d.shape[0]
    n_in = len(ys)
    in_specs = [pl.BlockSpec((TOK_TILE, D_MODEL), lambda i: (i, 0))]
    in_specs += [pl.BlockSpec((TOK_TILE, y.shape[1]), lambda i: (i, 0)) for y in ys]
    in_specs += [pl.BlockSpec(w.shape, lambda i: (0, 0)) for w in ws_bf16]
    return pl.pallas_call(
        functools.partial(_proj_out_kernel, n_in=n_in),
        out_shape=jax.ShapeDtypeStruct((m, D_MODEL), F32),
        grid=(m // TOK_TILE,),
        in_specs=in_specs,
        out_specs=pl.BlockSpec((TOK_TILE, D_MODEL), lambda i: (i, 0)),
        compiler_params=pltpu.CompilerParams(dimension_semantics=("parallel",),
                                             vmem_limit_bytes=VMEM_LIMIT),
        name="proj_out",
    )(x2d, *ys, *ws_bf16)


def _s5_kernel(*refs, reverse, final):
    if final:
        (u_ref, wb_ref, wc_ref, ein_r, ein_i, eout_r, eout_i, cin_r, cin_i,
         yprev_ref, z_ref, d_ref, gw_ref, gb_ref,
         o_ref, hin_r, hin_i, h_r, h_i, car_r, car_i) = refs
    else:
        (u_ref, wb_ref, wc_ref, ein_r, ein_i, eout_r, eout_i, cin_r, cin_i,
         o_ref, hin_r, hin_i, h_r, h_i, car_r, car_i) = refs

    @pl.when(pl.program_id(1) == 0)
    def _():
        car_r[...] = jnp.zeros_like(car_r)
        car_i[...] = jnp.zeros_like(car_i)

    u = u_ref[0]
    ub16 = u.astype(BF16)
    for ub in range(4):
        bu = _dot(ub16[:, ub * 128:(ub + 1) * 128], wb_ref[ub])
        hin_r[:, ub * 512:(ub + 1) * 512] = bu[:, :512]
        hin_i[:, ub * 512:(ub + 1) * 512] = bu[:, 512:]

    row = lax.broadcasted_iota(jnp.int32, (S5_SUB, S5_SUB), 0)
    col = lax.broadcasted_iota(jnp.int32, (S5_SUB, S5_SUB), 1)
    tri = jnp.where((col >= row) if reverse else (col <= row), 1.0, 0.0).astype(BF16)
    n_sub = S5_TILE // S5_SUB
    last = 0 if reverse else S5_SUB - 1

    cr = car_r[...]
    ci = car_i[...]
    for jj in range(n_sub):
        j = (n_sub - 1 - jj) if reverse else jj
        rows = slice(j * S5_SUB, (j + 1) * S5_SUB)
        xr = hin_r[rows, :]
        xi = hin_i[rows, :]
        er = ein_r[...]
        ei = ein_i[...]
        sr = (xr * er - xi * ei).astype(BF16)
        si = (xr * ei + xi * er).astype(BF16)
        add_r = cin_r[...] * cr - cin_i[...] * ci
        add_i = cin_r[...] * ci + cin_i[...] * cr
        cum_r = _dot(tri, sr) + add_r
        cum_i = _dot(tri, si) + add_i
        fr = eout_r[...]
        fi = eout_i[...]
        hr = cum_r * fr - cum_i * fi
        hi = cum_r * fi + cum_i * fr
        h_r[rows, :] = hr
        h_i[rows, :] = hi
        cr = hr[last:last + 1, :]
        ci = hi[last:last + 1, :]
    car_r[...] = cr
    car_i[...] = ci

    ys = []
    for ub in range(4):
        hcat = jnp.concatenate([h_r[:, ub * 512:(ub + 1) * 512],
                                h_i[:, ub * 512:(ub + 1) * 512]], axis=1).astype(BF16)
        ys.append(_dot(hcat, wc_ref[ub]))
    y = jnp.concatenate(ys, axis=1)
    if not final:
        o_ref[0] = y
        return
    y = y + yprev_ref[0] + d_ref[...] * u
    y = jax.nn.gelu(y)
    gate = _dot(y.astype(BF16), gw_ref[...]) + gb_ref[...]
    y = y * _sigmoid(gate)
    o_ref[0] = y * _silu(z_ref[0])


def s5_sweep(u, tabs, reverse, final_args=None):
    b, l, _ = u.shape
    nt = l // S5_TILE
    final = final_args is not None
    if reverse:
        tok = lambda bi, i: (bi, nt - 1 - i, 0)
    else:
        tok = lambda bi, i: (bi, i, 0)
    c2 = lambda bi, i: (0, 0)
    c3 = lambda bi, i: (0, 0, 0)
    wb, wc, ein_r, ein_i, eout_r, eout_i, cin_r, cin_i = tabs
    in_specs = [pl.BlockSpec((1, S5_TILE, S5_WIDTH), tok),
                pl.BlockSpec(wb.shape, c3), pl.BlockSpec(wc.shape, c3)]
    in_specs += [pl.BlockSpec(t.shape, c2) for t in (ein_r, ein_i, eout_r, eout_i, cin_r, cin_i)]
    args = [u, wb, wc, ein_r, ein_i, eout_r, eout_i, cin_r, cin_i]
    if final:
        yprev, z, d, gw, gb = final_args
        in_specs += [pl.BlockSpec((1, S5_TILE, S5_WIDTH), tok),
                     pl.BlockSpec((1, S5_TILE, S5_WIDTH), tok),
                     pl.BlockSpec(d.shape, c2), pl.BlockSpec(gw.shape, c2),
                     pl.BlockSpec(gb.shape, c2)]
        args += [yprev, z, d, gw, gb]
    lanes = S5_LANES
    return pl.pallas_call(
        functools.partial(_s5_kernel, reverse=reverse, final=final),
        out_shape=jax.ShapeDtypeStruct((b, l, S5_WIDTH), F32),
        grid=(b, nt),
        in_specs=in_specs,
        out_specs=pl.BlockSpec((1, S5_TILE, S5_WIDTH), tok),
        scratch_shapes=[pltpu.VMEM((S5_TILE, lanes), F32), pltpu.VMEM((S5_TILE, lanes), F32),
                        pltpu.VMEM((S5_TILE, lanes), F32), pltpu.VMEM((S5_TILE, lanes), F32),
                        pltpu.VMEM((1, lanes), F32), pltpu.VMEM((1, lanes), F32)],
        compiler_params=pltpu.CompilerParams(dimension_semantics=("arbitrary", "arbitrary"),
                                             vmem_limit_bytes=VMEM_LIMIT),
        name="s5_final" if final else "s5_sweep",
    )(*args)


def _s5_tables(a_re, a_im, log_dt, b_re, b_im, c_re, c_im, reverse):
    g, n = S5_GROUPS, S5_STATE
    dt = jnp.exp(log_dt.astype(F32))[:, None]
    ar = a_re.astype(F32)
    ai = a_im.astype(F32)
    mag = jnp.exp(ar * dt)
    lr = mag * jnp.cos(ai * dt)
    li = mag * jnp.sin(ai * dt)
    den = ar * ar + ai * ai
    nr = lr - 1.0
    qr = (nr * ar + li * ai) / den
    qi = (li * ar - nr * ai) / den
    br = b_re.astype(F32)
    bi = b_im.astype(F32)
    bbr = qr[..., None] * br - qi[..., None] * bi
    bbi = qr[..., None] * bi + qi[..., None] * br
    eye = jnp.eye(8, dtype=F32)

    def in_blocks(t):
        t = t.transpose(0, 2, 1).reshape(4, 8, S5_GROUP, n)
        return jnp.einsum('ugcn,gh->ugchn', t, eye).reshape(4, 128, 512)

    wb = jnp.concatenate([in_blocks(bbr), in_blocks(bbi)], axis=2).astype(BF16)

    def out_blocks(t):
        t = t.astype(F32).transpose(0, 2, 1).reshape(4, 8, n, S5_GROUP)
        return jnp.einsum('ugnc,gh->ugnhc', t, eye).reshape(4, 512, 128)

    wc = jnp.concatenate([out_blocks(c_re), -out_blocks(c_im)], axis=1).astype(BF16)

    la = (ar * dt).reshape(1, -1)
    th = (ai * dt).reshape(1, -1)
    pos = jnp.arange(S5_SUB, dtype=F32)[:, None]

    def powers(e):
        m = jnp.exp(e * la)
        return m * jnp.cos(e * th), m * jnp.sin(e * th)

    if reverse:
        ein = powers(pos - S5_MID)
        eout = powers(S5_MID - pos)
        cin = powers(jnp.full((1, 1), S5_SUB - S5_MID, F32))
    else:
        ein = powers(S5_MID - pos)
        eout = powers(pos - S5_MID)
        cin = powers(jnp.full((1, 1), S5_MID + 1.0, F32))
    return (wb, wc, ein[0], ein[1], eout[0], eout[1], cin[0], cin[1])


def _rw_prep_kernel(x_ref, xp_ref, xn_ref, mu_ref, aup_ref, wup0_ref, wup1_ref,
                    a0_ref, w00_ref, w01_ref, kk_ref, ka_ref, ones_ref,
                    r_o, k_o, v_o, kk_o, b_o, l0_o, l1_o, *, tile):
    i = pl.program_id(1)
    nt = pl.num_programs(1)
    x = x_ref[0]
    rowi = lax.broadcasted_iota(jnp.int32, x.shape, 0)
    prev_row = xp_ref[0][7:8, :] * jnp.where(i > 0, 1.0, 0.0)
    next_row = xn_ref[0][0:1, :] * jnp.where(i < nt - 1, 1.0, 0.0)
    x_prev = jnp.where(rowi == 0, prev_row, pltpu.roll(x, 1, 0))
    x_next = jnp.where(rowi == tile - 1, next_row, pltpu.roll(x, tile - 1, 0))
    nb = 0.5 * (x_prev + x_next)
    xs = x + mu_ref[...] * (nb - x)
    r = xs[:, 0:512]
    k = xs[:, 512:1024]
    v = xs[:, 1024:1536]
    lt = xs[:, 1536:1664]
    a = _sigmoid(a0_ref[...] + _dot(lt.astype(BF16), aup_ref[...]))
    tw = jnp.tanh(lt).astype(BF16)
    c = math.exp(-0.5)
    l0 = -c * _sigmoid(w00_ref[...] + _dot(tw, wup0_ref[...]))
    l1 = -c * _sigmoid(w01_ref[...] + _dot(tw, wup1_ref[...]))
    kk = k * kk_ref[...]
    s1, s2 = _split2(kk * kk)
    ss = _dot(s1, ones_ref[...]) + _dot(s2, ones_ref[...])
    kk = kk / jnp.maximum(jnp.sqrt(ss), 1e-12)
    r_o[0] = r
    k_o[0] = k * (1.0 + (a - 1.0) * ka_ref[...])
    v_o[0] = v
    kk_o[0] = kk
    b_o[0] = kk * a
    l0_o[0] = l0
    l1_o[0] = l1


def rw_prep(h_rw, p, tile=256):
    b, l, w = h_rw.shape
    nt = l // tile
    tb = tile // 8
    tok = lambda bi, i: (bi, i, 0)
    c2 = lambda bi, i: (0, 0)
    params = [p['mu'], p['aup'], p['wup0'], p['wup1'], p['a0'], p['w00'], p['w01'],
              p['kk'], p['ka'], p['ones']]
    in_specs = [pl.BlockSpec((1, tile, w), tok),
                pl.BlockSpec((1, 8, w), lambda bi, i: (bi, jnp.maximum(i * tb - 1, 0), 0)),
                pl.BlockSpec((1, 8, w), lambda bi, i: (bi, jnp.minimum((i + 1) * tb, nt * tb - 1), 0))]
    in_specs += [pl.BlockSpec(a.shape, c2) for a in params]
    return pl.pallas_call(
        functools.partial(_rw_prep_kernel, tile=tile),
        out_shape=[jax.ShapeDtypeStruct((b, l, RW_WIDTH), F32)] * 7,
        grid=(b, nt),
        in_specs=in_specs,
        out_specs=[pl.BlockSpec((1, tile, RW_WIDTH), tok)] * 7,
        compiler_params=pltpu.CompilerParams(dimension_semantics=("parallel", "parallel"),
                                             vmem_limit_bytes=VMEM_LIMIT),
        name="rw_prep",
    )(h_rw, h_rw, h_rw, *params)


def _rw_block_diag(x, bmask):
    xb = x.astype(BF16)
    return jnp.where(bmask, jnp.concatenate([xb, xb, xb, xb], axis=0), jnp.zeros((), BF16))


def _lockstep(gens):
    gens = list(gens)
    results = [None] * len(gens)
    live = list(range(len(gens)))
    while live:
        still = []
        for i in live:
            try:
                next(gens[i])
                still.append(i)
            except StopIteration as done:
                results[i] = done.value
        live = still
    return results


def _rw_chunk_prepare(r, k, v, kk, b, lw, masks, reverse):
    tri, strict, incl, bmask, eye = masks
    c = RW_CHUNK
    bd = functools.partial(_rw_block_diag, bmask=bmask)

    l1, l2, l3 = _split3(lw)
    cl = _dot(tri, l1) + _dot(tri, l2) + _dot(tri, l3)
    yield
    cl_end = cl[0:1, :] if reverse else cl[c - 1:c, :]
    e_neg = jnp.exp(-cl)
    e_end = jnp.exp(cl_end - cl)
    kkt = kk * jnp.exp(cl - lw)
    rt = r * jnp.exp(cl)
    lhs = jnp.concatenate([kkt, rt], axis=0).astype(BF16)
    ab = _dot_nt(lhs, bd(b * e_neg))
    ak = _dot_nt(lhs, bd(k * e_neg))
    yield
    a_bb = jnp.where(strict, ab[:c], 0.0)
    a_rb = jnp.where(incl, ab[c:], 0.0).astype(BF16)
    a_bk = jnp.where(strict, ak[:c], 0.0).astype(BF16)
    a_rk = jnp.where(incl, ak[c:], 0.0).astype(BF16)
    bdv = bd(v)

    pw = -a_bb
    t = eye + pw
    pw = _dot(pw.astype(BF16), bd(pw))
    av = _dot(a_bk, bdv)
    yield
    for _ in range(4):
        res = _dot(jnp.concatenate([pw, t], axis=0).astype(BF16), bd(pw))
        yield
        pw = res[:c]
        t = t + res[c:]
    inv = (t + _dot(t.astype(BF16), bd(pw))).astype(BF16)
    yield
    un = -_dot(inv, jnp.concatenate([bd(av), bd(kkt)], axis=1))
    yield
    u0 = un[:, :RW_GROUP]
    ng = un[:, RW_GROUP:].astype(BF16)
    bde = (b * e_end).astype(BF16)
    y0 = _dot(jnp.concatenate([a_rb, a_rk], axis=1), jnp.concatenate([bd(u0), bdv], axis=0))
    rq = (rt + _dot(a_rb, bd(ng))).astype(BF16)
    trans = _dot_tn(ng, bde)
    drive = _dot_tn(jnp.concatenate([u0.astype(BF16), v.astype(BF16)], axis=0),
                    jnp.concatenate([bde, (k * e_end).astype(BF16)], axis=0))
    yield
    trans = jnp.where(bmask, trans, 0.0).astype(BF16)
    drive = jnp.where(bmask, drive, 0.0)
    return rq, y0, trans, drive, jnp.exp(cl_end)


def _rw_apply(tiles, preps, s_ref, o_ref):
    for (rows, g, lanes), (rq, y0, trans, drive, p_end) in zip(tiles, preps):
        s0 = s_ref[g]
        s0b = s0.astype(BF16)
        s_ref[g] = s0 * p_end + _dot(s0b, trans) + drive
        o_ref[0, rows, lanes] = y0 + _dot_nt(rq, s0b)
        if g == RW_WIDTH // RW_GROUP - 1:
            yield


def _rw_masks(reverse):
    c, g = RW_CHUNK, RW_GROUP
    r2 = lax.broadcasted_iota(jnp.int32, (c, c), 0)
    c2 = lax.broadcasted_iota(jnp.int32, (c, c), 1)
    tri = jnp.where((c2 >= r2) if reverse else (c2 <= r2), 1.0, 0.0).astype(BF16)
    t = lax.broadcasted_iota(jnp.int32, (c, g), 0)
    s = lax.broadcasted_iota(jnp.int32, (c, g), 1) & (c - 1)
    strict = (s > t) if reverse else (s < t)
    incl = (s >= t) if reverse else (s <= t)
    eye = jnp.where(s == t, 1.0, 0.0).astype(F32)
    bi = lax.broadcasted_iota(jnp.int32, (g, g), 0) >> 6
    bj = lax.broadcasted_iota(jnp.int32, (g, g), 1) >> 6
    return tri, strict, incl, bi == bj, eye


def _rw_scan_kernel(*refs, reverse, final):
    if final:
        (r_ref, k_ref, v_ref, kk_ref, b_ref, lw_ref, yprev_ref, z_ref,
         rk_ref, lng_ref, lnb_ref, ones_ref, o_ref, s_ref) = refs
    else:
        (r_ref, k_ref, v_ref, kk_ref, b_ref, lw_ref, o_ref, s_ref) = refs

    @pl.when(pl.program_id(1) == 0)
    def _():
        s_ref[...] = jnp.zeros_like(s_ref)

    masks = _rw_masks(reverse)
    n_chunks = RW_TILE // RW_CHUNK
    order = list(range(n_chunks - 1, -1, -1) if reverse else range(n_chunks))
    n_groups = RW_WIDTH // RW_GROUP
    pending = None
    for w in range(0, n_chunks, RW_WAVE):
        tiles = [(slice(ci * RW_CHUNK, (ci + 1) * RW_CHUNK), g, slice(g * RW_GROUP, (g + 1) * RW_GROUP))
                 for ci in order[w:w + RW_WAVE] for g in range(n_groups)]
        gens = [_rw_chunk_prepare(r_ref[0, rows, lanes], k_ref[0, rows, lanes], v_ref[0, rows, lanes],
                                  kk_ref[0, rows, lanes], b_ref[0, rows, lanes], lw_ref[0, rows, lanes],
                                  masks, reverse)
                for rows, g, lanes in tiles]
        if pending is not None:
            gens.append(_rw_apply(*pending, s_ref, o_ref))
        preps = _lockstep(gens)[:len(tiles)]
        pending = (tiles, preps)
    _lockstep([_rw_apply(*pending, s_ref, o_ref)])

    if not final:
        return
    ones = ones_ref[...]
    inv_n = 1.0 / RW_HEAD

    def head_sum(x):
        x1, x2 = _split2(x)
        return _dot(x1, ones) + _dot(x2, ones)

    y = o_ref[0] + yprev_ref[0]
    mean = head_sum(y) * inv_n
    d = y - mean
    var = head_sum(d * d) * inv_n
    yn = d * lax.rsqrt(var + RW_LN_EPS) * lng_ref[...] + lnb_ref[...]
    bonus = head_sum(r_ref[0] * k_ref[0] * rk_ref[...]) * v_ref[0]
    o_ref[0] = (yn + bonus) * _silu(z_ref[0])


def rw_sweep(seqs, reverse, final_args=None):
    b, l, _ = seqs[0].shape
    nt = l // RW_TILE
    final = final_args is not None
    if reverse:
        tok = lambda bi, i: (bi, nt - 1 - i, 0)
    else:
        tok = lambda bi, i: (bi, i, 0)
    c2 = lambda bi, i: (0, 0)
    tok_spec = pl.BlockSpec((1, RW_TILE, RW_WIDTH), tok)
    in_specs = [tok_spec] * 6
    args = list(seqs)
    if final:
        yprev, z, rk, lng, lnb, ones = final_args
        in_specs += [tok_spec, tok_spec]
        in_specs += [pl.BlockSpec(a.shape, c2) for a in (rk, lng, lnb, ones)]
        args += [yprev, z, rk, lng, lnb, ones]
    return pl.pallas_call(
        functools.partial(_rw_scan_kernel, reverse=reverse, final=final),
        out_shape=jax.ShapeDtypeStruct((b, l, RW_WIDTH), F32),
        grid=(b, nt),
        in_specs=in_specs,
        out_specs=tok_spec,
        scratch_shapes=[pltpu.VMEM((RW_WIDTH // RW_GROUP, RW_GROUP, RW_GROUP), F32)],
        compiler_params=pltpu.CompilerParams(dimension_semantics=("arbitrary", "arbitrary"),
                                             vmem_limit_bytes=VMEM_LIMIT),
        name="rw_final" if final else "rw_sweep",
    )(*args)


def _attn_kernel(sink_ref, q_ref, kp_ref, kc_ref, kn_ref, vp_ref, vc_ref, vn_ref, z_ref,
                 qn_ref, kn_g_ref, o_ref):
    i = pl.program_id(1)
    nb = pl.num_programs(1)
    blk = AT_BLOCK
    iq = lax.broadcasted_iota(jnp.int32, (blk, 3 * blk), 0)
    jk = lax.broadcasted_iota(jnp.int32, (blk, 3 * blk), 1)
    rel = blk + iq - jk
    absrel = jnp.abs(rel)
    key_lo = jnp.where(i > 0, 0, blk)
    key_hi = jnp.where(i < nb - 1, 3 * blk, 2 * blk)
    valid = (absrel <= blk) & (jk >= key_lo) & (jk < key_hi)
    absrel_f = absrel.astype(F32)
    qg = qn_ref[...]
    kg = kn_g_ref[...]
    scale = AT_HEAD_DIM ** -0.5

    def rms(t, g):
        return t * lax.rsqrt(jnp.mean(t * t, axis=-1, keepdims=True) + EPS) * g

    q_all = q_ref[0]
    outs = []
    for kvh in range(AT_KV_HEADS):
        ks = slice(kvh * AT_HEAD_DIM, (kvh + 1) * AT_HEAD_DIM)
        kband = jnp.concatenate([kp_ref[0][:, ks], kc_ref[0][:, ks], kn_ref[0][:, ks]], axis=0)
        vband = jnp.concatenate([vp_ref[0][:, ks], vc_ref[0][:, ks], vn_ref[0][:, ks]], axis=0)
        kband = rms(kband, kg).astype(BF16)
        vband = vband.astype(BF16)
        for g in range(AT_GROUP):
            h = kvh * AT_GROUP + g
            qh = rms(q_all[:, h * AT_HEAD_DIM:(h + 1) * AT_HEAD_DIM], qg) * scale
            s = _dot_nt(qh.astype(BF16), kband)
            slope = 2.0 ** (-8.0 * (h + 1) / AT_HEADS)
            s = jnp.where(valid, s - slope * absrel_f, NEG_INF)
            sink = sink_ref[h]
            m = jnp.maximum(jnp.max(s, axis=-1, keepdims=True), sink)
            p = jnp.exp(s - m)
            den = jnp.sum(p, axis=-1, keepdims=True) + jnp.exp(sink - m)
            outs.append(_dot(p.astype(BF16), vband) / den)
    o = jnp.concatenate(outs, axis=1)
    o_ref[0] = o * _silu(z_ref[0])


def attention(q, k, v, z, sink, qn, kn):
    b, l, _ = q.shape
    nb = l // AT_BLOCK
    cur = lambda bi, i, s: (bi, i, 0)
    prv = lambda bi, i, s: (bi, jnp.maximum(i - 1, 0), 0)
    nxt = lambda bi, i, s: (bi, jnp.minimum(i + 1, nb - 1), 0)
    c2 = lambda bi, i, s: (0, 0)
    kv = lambda f: pl.BlockSpec((1, AT_BLOCK, AT_KV_WIDTH), f)
    wide = pl.BlockSpec((1, AT_BLOCK, AT_WIDTH), cur)
    grid_spec = pltpu.PrefetchScalarGridSpec(
        num_scalar_prefetch=1,
        grid=(b, nb),
        in_specs=[wide, kv(prv), kv(cur), kv(nxt), kv(prv), kv(cur), kv(nxt), wide,
                  pl.BlockSpec((1, AT_HEAD_DIM), c2), pl.BlockSpec((1, AT_HEAD_DIM), c2)],
        out_specs=wide,
    )
    return pl.pallas_call(
        _attn_kernel,
        out_shape=jax.ShapeDtypeStruct((b, l, AT_WIDTH), F32),
        grid_spec=grid_spec,
        compiler_params=pltpu.CompilerParams(dimension_semantics=("parallel", "parallel"),
                                             vmem_limit_bytes=VMEM_LIMIT),
        name="attn",
    )(sink, q, k, k, k, v, v, v, z, qn.reshape(1, -1), kn.reshape(1, -1))


def _block_ones(width, block):
    i = jnp.arange(width) // block
    return (i[:, None] == i[None, :]).astype(BF16)


def _even_params(j, ev):
    g = lambda name: ev[name][j]
    row = lambda a: a.astype(F32).reshape(1, -1)
    p = {}
    p['norm'] = g('norm')
    p['w_in'] = g('w_in').astype(BF16)
    p['w_out_a'] = g('w_out')[:S5_WIDTH].astype(BF16)
    p['w_out_b'] = g('w_out')[S5_WIDTH:].astype(BF16)
    p['s5'] = [
        _s5_tables(g('s5_a_re')[d], g('s5_a_im')[d], g('s5_log_dt')[d], g('s5_b_re')[d],
                   g('s5_b_im')[d], g('s5_c_re')[d], g('s5_c_im')[d], reverse=bool(d))
        for d in range(2)]
    p['s5_d'] = row(g('s5_d'))
    p['glu_w'] = g('s5_glu_w').astype(BF16)
    p['glu_b'] = row(g('s5_glu_b'))
    zeros64 = jnp.zeros((64, RW_WIDTH), F32)
    p['mu'] = row(g('rw_mu'))
    p['aup'] = jnp.concatenate([zeros64, g('rw_a_up').astype(F32)], axis=0).astype(BF16)
    p['wup0'] = jnp.concatenate([g('rw_w_up')[0].astype(F32), zeros64], axis=0).astype(BF16)
    p['wup1'] = jnp.concatenate([g('rw_w_up')[1].astype(F32), zeros64], axis=0).astype(BF16)
    p['a0'] = row(g('rw_a0'))
    p['w00'] = row(g('rw_w0')[0])
    p['w01'] = row(g('rw_w0')[1])
    p['kk'] = row(g('rw_k_k'))
    p['ka'] = row(g('rw_k_a'))
    p['ones'] = _block_ones(RW_WIDTH, RW_HEAD)
    p['rk'] = row(g('rw_r_k'))
    p['lng'] = row(g('rw_ln_g'))
    p['lnb'] = row(g('rw_ln_b'))
    return p


def _even_layer(x, p):
    b, l, _ = x.shape
    x2 = x.reshape(b * l, D_MODEL)
    u, z_s5, h_rw, z_rw = proj_in(x2, p['norm'], p['w_in'],
                                  (S5_WIDTH, S5_WIDTH, RW_SHIFTED, RW_WIDTH))
    u = u.reshape(b, l, -1)
    z_s5 = z_s5.reshape(b, l, -1)
    h_rw = h_rw.reshape(b, l, -1)
    z_rw = z_rw.reshape(b, l, -1)

    y_rev = s5_sweep(u, p['s5'][1], reverse=True)
    ya = s5_sweep(u, p['s5'][0], reverse=False,
                  final_args=(y_rev, z_s5, p['s5_d'], p['glu_w'], p['glu_b']))

    r, k, v, kk, bb, l0, l1 = rw_prep(h_rw, p)
    yb_rev = rw_sweep((r, k, v, kk, bb, l1), reverse=True)
    yb = rw_sweep((r, k, v, kk, bb, l0), reverse=False,
                  final_args=(yb_rev, z_rw, p['rk'], p['lng'], p['lnb'], p['ones']))

    out = proj_out(x2, [ya.reshape(b * l, -1), yb.reshape(b * l, -1)],
                   [p['w_out_a'], p['w_out_b']])
    return out.reshape(b, l, D_MODEL)


def _odd_layer(x, p):
    b, l, _ = x.shape
    x2 = x.reshape(b * l, D_MODEL)
    q, k, v, z = proj_in(x2, p['norm'], p['w_in'], (AT_WIDTH, AT_KV_WIDTH, AT_KV_WIDTH, AT_WIDTH))
    sh = lambda t: t.reshape(b, l, -1)
    o = attention(sh(q), sh(k), sh(v), sh(z), p['sink'], p['qn'], p['kn'])
    out = proj_out(x2, [o.reshape(b * l, -1)], [p['w_out']])
    return out.reshape(b, l, D_MODEL)


def kernel(x_prompt, x_sample, ev_norm, ev_w_in, s5_a_re, s5_a_im, s5_log_dt, s5_b_re, s5_b_im, s5_c_re, s5_c_im, s5_d, s5_glu_w, s5_glu_b, rw_mu, rw_w0, rw_w_up, rw_a0, rw_a_up, rw_k_k, rw_k_a, rw_r_k, rw_ln_g, rw_ln_b, ev_w_out, od_norm, od_w_in, at_q_norm, at_k_norm, at_sink, od_w_out):
    ev = {'norm': ev_norm, 'w_in': ev_w_in, 's5_a_re': s5_a_re, 's5_a_im': s5_a_im,
          's5_log_dt': s5_log_dt, 's5_b_re': s5_b_re, 's5_b_im': s5_b_im,
          's5_c_re': s5_c_re, 's5_c_im': s5_c_im, 's5_d': s5_d, 's5_glu_w': s5_glu_w,
          's5_glu_b': s5_glu_b, 'rw_mu': rw_mu, 'rw_w0': rw_w0, 'rw_w_up': rw_w_up,
          'rw_a0': rw_a0, 'rw_a_up': rw_a_up, 'rw_k_k': rw_k_k, 'rw_k_a': rw_k_a,
          'rw_r_k': rw_r_k, 'rw_ln_g': rw_ln_g, 'rw_ln_b': rw_ln_b, 'w_out': ev_w_out}
    n_even = ev_norm.shape[0]
    n_odd = od_norm.shape[0]
    even = [_even_params(j, ev) for j in range(n_even)]
    odd = [{'norm': od_norm[j], 'w_in': od_w_in[j].astype(BF16),
            'qn': at_q_norm[j].astype(F32), 'kn': at_k_norm[j].astype(F32),
            'sink': at_sink[j].astype(F32), 'w_out': od_w_out[j].astype(BF16)}
           for j in range(n_odd)]

    def trunk(x):
        for layer in range(n_even + n_odd):
            j = layer // 2
            x = _even_layer(x, even[j]) if layer % 2 == 0 else _odd_layer(x, odd[j])
        return x

    return (trunk(x_prompt), trunk(x_sample))
```

```python
import functools
import math

import jax
import jax.numpy as jnp
from jax import lax
from jax.experimental import pallas as pl
from jax.experimental.pallas import tpu as pltpu

F32 = jnp.float32
BF16 = jnp.bfloat16

D_MODEL = 1024
EPS = 1e-6

S5_WIDTH = 512
S5_GROUP = 16
S5_GROUPS = 32
S5_STATE = 64
S5_LANES = S5_GROUPS * S5_STATE
S5_SUB = 16
S5_MID = 7.5
S5_TILE = 512
S5_PART = 128

RW_WIDTH = 512
RW_HEAD = 64
RW_SHIFTED = 3 * RW_WIDTH + 128
RW_LN_EPS = 64e-5
RW_CHUNK = 64
RW_TILE = 512
RW_WAVE = 4
RW_GROUP = 256

AT_HEADS = 16
AT_KV_HEADS = 4
AT_GROUP = 4
AT_HEAD_DIM = 64
AT_BLOCK = 128
AT_WIDTH = 1024
AT_KV_WIDTH = 256
NEG_INF = -1e30

TOK_TILE = 512
VMEM_LIMIT = 48 * 1024 * 1024


def _dot(a, b):
    return jnp.dot(a, b, preferred_element_type=F32)


def _dot_nt(a, b):
    return lax.dot_general(a, b, (((1,), (1,)), ((), ())), preferred_element_type=F32)


def _dot_tn(a, b):
    return lax.dot_general(a, b, (((0,), (0,)), ((), ())), preferred_element_type=F32)


def _lockstep(gens):
    gens = list(gens)
    results = [None] * len(gens)
    live = list(range(len(gens)))
    while live:
        still = []
        for i in live:
            try:
                next(gens[i])
                still.append(i)
            except StopIteration as done:
                results[i] = done.value
        live = still
    return results


def _split3(x):
    x1 = x.astype(BF16)
    r1 = x - x1.astype(F32)
    x2 = r1.astype(BF16)
    x3 = (r1 - x2.astype(F32)).astype(BF16)
    return x1, x2, x3


def _split2(x):
    x1 = x.astype(BF16)
    x2 = (x - x1.astype(F32)).astype(BF16)
    return x1, x2


def _sigmoid(x):
    return 1.0 / (1.0 + jnp.exp(-x))


def _silu(x):
    return x * _sigmoid(x)


def _proj_in_kernel(x_ref, g_ref, w_ref, *out_refs, splits):
    x = x_ref[...]
    ms = jnp.mean(x * x, axis=-1, keepdims=True)
    h = (x * lax.rsqrt(ms + EPS) * g_ref[...]).astype(BF16)
    off = 0
    for o_ref, n in zip(out_refs, splits):
        o_ref[...] = _dot(h, w_ref[:, off:off + n])
        off += n


def proj_in(x2d, g, w_bf16, splits):
    m = x2d.shape[0]
    n = w_bf16.shape[1]
    assert sum(splits) == n and m % TOK_TILE == 0
    return pl.pallas_call(
        functools.partial(_proj_in_kernel, splits=splits),
        out_shape=[jax.ShapeDtypeStruct((m, s), F32) for s in splits],
        grid=(m // TOK_TILE,),
        in_specs=[pl.BlockSpec((TOK_TILE, D_MODEL), lambda i: (i, 0)),
                  pl.BlockSpec((1, D_MODEL), lambda i: (0, 0)),
                  pl.BlockSpec((D_MODEL, n), lambda i: (0, 0))],
        out_specs=[pl.BlockSpec((TOK_TILE, s), lambda i: (i, 0)) for s in splits],
        compiler_params=pltpu.CompilerParams(dimension_semantics=("parallel",),
                                             vmem_limit_bytes=VMEM_LIMIT),
        name="proj_in",
    )(x2d, g.reshape(1, D_MODEL), w_bf16)


def _proj_out_kernel(*refs, n_in):
    x_ref = refs[0]
    y_refs = refs[1:1 + n_in]
    w_refs = refs[1 + n_in:1 + 2 * n_in]
    o_ref = refs[1 + 2 * n_in]
    acc = x_ref[...]
    for y_ref, w_ref in zip(y_refs, w_refs):
        acc = acc + _dot(y_ref[...].astype(BF16), w_ref[...])
    o_ref[...] = acc


def proj_out(x2d, ys, ws_bf16):
    m = x2d.shape[0]
    n_in = len(ys)
    in_specs = [pl.BlockSpec((TOK_TILE, D_MODEL), lambda i: (i, 0))]
    in_specs += [pl.BlockSpec((TOK_TILE, y.shape[1]), lambda i: (i, 0)) for y in ys]
    in_specs += [pl.BlockSpec(w.shape, lambda i: (0, 0)) for w in ws_bf16]
    return pl.pallas_call(
        functools.partial(_proj_out_kernel, n_in=n_in),
        out_shape=jax.ShapeDtypeStruct((m, D_MODEL), F32),
        grid=(m // TOK_TILE,),
        in_specs=in_specs,
        out_specs=pl.BlockSpec((TOK_TILE, D_MODEL), lambda i: (i, 0)),
        compiler_params=pltpu.CompilerParams(dimension_semantics=("parallel",),
                                             vmem_limit_bytes=VMEM_LIMIT),
        name="proj_out",
    )(x2d, *ys, *ws_bf16)


def _s5_kernel(*refs, reverse, final):
    if final:
        (u_ref, wb_ref, wc_ref, ein_r, ein_i, eout_r, eout_i, cin_r, cin_i,
         yprev_ref, z_ref, d_ref, gw_ref, gb_ref,
         o_ref, hin_r, hin_i, h_r, h_i, car_r, car_i) = refs
    else:
        (u_ref, wb_ref, wc_ref, ein_r, ein_i, eout_r, eout_i, cin_r, cin_i,
         o_ref, hin_r, hin_i, h_r, h_i, car_r, car_i) = refs

    @pl.when(pl.program_id(1) == 0)
    def _():
        car_r[...] = jnp.zeros_like(car_r)
        car_i[...] = jnp.zeros_like(car_i)

    row = lax.broadcasted_iota(jnp.int32, (S5_SUB, S5_SUB), 0)
    col = lax.broadcasted_iota(jnp.int32, (S5_SUB, S5_SUB), 1)
    tri = jnp.where((col >= row) if reverse else (col <= row), 1.0, 0.0).astype(BF16)
    last = 0 if reverse else S5_SUB - 1
    n_part = S5_TILE // S5_PART
    n_sub = S5_PART // S5_SUB
    carry = [car_r[...], car_i[...]]

    def part_rows(n):
        n = (n_part - 1 - n) if reverse else n
        return n * S5_PART

    def project_in(n):
        r0 = part_rows(n)
        ub16 = u_ref[0, r0:r0 + S5_PART, :].astype(BF16)
        for ub in range(4):
            bu = _dot(ub16[:, ub * 128:(ub + 1) * 128], wb_ref[ub])
            hin_r[r0:r0 + S5_PART, ub * 512:(ub + 1) * 512] = bu[:, :512]
            hin_i[r0:r0 + S5_PART, ub * 512:(ub + 1) * 512] = bu[:, 512:]
            yield

    def scan(n):
        r0 = part_rows(n)
        cr, ci = carry
        for jj in range(n_sub):
            j = (n_sub - 1 - jj) if reverse else jj
            rows = slice(r0 + j * S5_SUB, r0 + (j + 1) * S5_SUB)
            xr = hin_r[rows, :]
            xi = hin_i[rows, :]
            er = ein_r[...]
            ei = ein_i[...]
            sr = (xr * er - xi * ei).astype(BF16)
            si = (xr * ei + xi * er).astype(BF16)
            add_r = cin_r[...] * cr - cin_i[...] * ci
            add_i = cin_r[...] * ci + cin_i[...] * cr
            cum_r = _dot(tri, sr) + add_r
            cum_i = _dot(tri, si) + add_i
            fr = eout_r[...]
            fi = eout_i[...]
            hr = cum_r * fr - cum_i * fi
            hi = cum_r * fi + cum_i * fr
            h_r[rows, :] = hr
            h_i[rows, :] = hi
            cr = hr[last:last + 1, :]
            ci = hi[last:last + 1, :]
            if jj % 2 == 1:
                yield
        carry[0], carry[1] = cr, ci

    def project_out(n):
        r0 = part_rows(n)
        for ub in range(4):
            hcat = jnp.concatenate([h_r[r0:r0 + S5_PART, ub * 512:(ub + 1) * 512],
                                    h_i[r0:r0 + S5_PART, ub * 512:(ub + 1) * 512]], axis=1)
            o_ref[0, r0:r0 + S5_PART, ub * 128:(ub + 1) * 128] = _dot(hcat.astype(BF16), wc_ref[ub])
            yield

    for step in range(n_part + 2):
        stage = []
        if step < n_part:
            stage.append(project_in(step))
        if 0 <= step - 1 < n_part:
            stage.append(scan(step - 1))
        if 0 <= step - 2 < n_part:
            stage.append(project_out(step - 2))
        _lockstep(stage)
    car_r[...] = carry[0]
    car_i[...] = carry[1]

    if not final:
        return
    u = u_ref[0]
    y = o_ref[0]
    y = y + yprev_ref[0] + d_ref[...] * u
    y = jax.nn.gelu(y)
    gate = _dot(y.astype(BF16), gw_ref[...]) + gb_ref[...]
    y = y * _sigmoid(gate)
    o_ref[0] = y * _silu(z_ref[0])


def s5_sweep(u, tabs, reverse, final_args=None):
    b, l, _ = u.shape
    nt = l // S5_TILE
    final = final_args is not None
    if reverse:
        tok = lambda bi, i: (bi, nt - 1 - i, 0)
    else:
        tok = lambda bi, i: (bi, i, 0)
    c2 = lambda bi, i: (0, 0)
    c3 = lambda bi, i: (0, 0, 0)
    wb, wc, ein_r, ein_i, eout_r, eout_i, cin_r, cin_i = tabs
    in_specs = [pl.BlockSpec((1, S5_TILE, S5_WIDTH), tok),
                pl.BlockSpec(wb.shape, c3), pl.BlockSpec(wc.shape, c3)]
    in_specs += [pl.BlockSpec(t.shape, c2) for t in (ein_r, ein_i, eout_r, eout_i, cin_r, cin_i)]
    args = [u, wb, wc, ein_r, ein_i, eout_r, eout_i, cin_r, cin_i]
    if final:
        yprev, z, d, gw, gb = final_args
        in_specs += [pl.BlockSpec((1, S5_TILE, S5_WIDTH), tok),
                     pl.BlockSpec((1, S5_TILE, S5_WIDTH), tok),
                     pl.BlockSpec(d.shape, c2), pl.BlockSpec(gw.shape, c2),
                     pl.BlockSpec(gb.shape, c2)]
        args += [yprev, z, d, gw, gb]
    lanes = S5_LANES
    return pl.pallas_call(
        functools.partial(_s5_kernel, reverse=reverse, final=final),
        out_shape=jax.ShapeDtypeStruct((b, l, S5_WIDTH), F32),
        grid=(b, nt),
        in_specs=in_specs,
        out_specs=pl.BlockSpec((1, S5_TILE, S5_WIDTH), tok),
        scratch_shapes=[pltpu.VMEM((S5_TILE, lanes), F32), pltpu.VMEM((S5_TILE, lanes), F32),
                        pltpu.VMEM((S5_TILE, lanes), F32), pltpu.VMEM((S5_TILE, lanes), F32),
                        pltpu.VMEM((1, lanes), F32), pltpu.VMEM((1, lanes), F32)],
        compiler_params=pltpu.CompilerParams(dimension_semantics=("arbitrary", "arbitrary"),
                                             vmem_limit_bytes=VMEM_LIMIT),
        name="s5_final" if final else "s5_sweep",
    )(*args)


def _s5_tables(a_re, a_im, log_dt, b_re, b_im, c_re, c_im, reverse):
    g, n = S5_GROUPS, S5_STATE
    dt = jnp.exp(log_dt.astype(F32))[:, None]
    ar = a_re.astype(F32)
    ai = a_im.astype(F32)
    mag = jnp.exp(ar * dt)
    lr = mag * jnp.cos(ai * dt)
    li = mag * jnp.sin(ai * dt)
    den = ar * ar + ai * ai
    nr = lr - 1.0
    qr = (nr * ar + li * ai) / den
    qi = (li * ar - nr * ai) / den
    br = b_re.astype(F32)
    bi = b_im.astype(F32)
    bbr = qr[..., None] * br - qi[..., None] * bi
    bbi = qr[..., None] * bi + qi[..., None] * br
    eye = jnp.eye(8, dtype=F32)

    def in_blocks(t):
        t = t.transpose(0, 2, 1).reshape(4, 8, S5_GROUP, n)
        return jnp.einsum('ugcn,gh->ugchn', t, eye).reshape(4, 128, 512)

    wb = jnp.concatenate([in_blocks(bbr), in_blocks(bbi)], axis=2).astype(BF16)

    def out_blocks(t):
        t = t.astype(F32).transpose(0, 2, 1).reshape(4, 8, n, S5_GROUP)
        return jnp.einsum('ugnc,gh->ugnhc', t, eye).reshape(4, 512, 128)

    wc = jnp.concatenate([out_blocks(c_re), -out_blocks(c_im)], axis=1).astype(BF16)

    la = (ar * dt).reshape(1, -1)
    th = (ai * dt).reshape(1, -1)
    pos = jnp.arange(S5_SUB, dtype=F32)[:, None]

    def powers(e):
        m = jnp.exp(e * la)
        return m * jnp.cos(e * th), m * jnp.sin(e * th)

    if reverse:
        ein = powers(pos - S5_MID)
        eout = powers(S5_MID - pos)
        cin = powers(jnp.full((1, 1), S5_SUB - S5_MID, F32))
    else:
        ein = powers(S5_MID - pos)
        eout = powers(pos - S5_MID)
        cin = powers(jnp.full((1, 1), S5_MID + 1.0, F32))
    return (wb, wc, ein[0], ein[1], eout[0], eout[1], cin[0], cin[1])


def _rw_prep_kernel(x_ref, xp_ref, xn_ref, mu_ref, aup_ref, wup0_ref, wup1_ref,
                    a0_ref, w00_ref, w01_ref, kk_ref, ka_ref, ones_ref,
                    r_o, k_o, v_o, kk_o, b_o, l0_o, l1_o, *, tile):
    i = pl.program_id(1)
    nt = pl.num_programs(1)
    x = x_ref[0]
    rowi = lax.broadcasted_iota(jnp.int32, x.shape, 0)
    prev_row = xp_ref[0][7:8, :] * jnp.where(i > 0, 1.0, 0.0)
    next_row = xn_ref[0][0:1, :] * jnp.where(i < nt - 1, 1.0, 0.0)
    x_prev = jnp.where(rowi == 0, prev_row, pltpu.roll(x, 1, 0))
    x_next = jnp.where(rowi == tile - 1, next_row, pltpu.roll(x, tile - 1, 0))
    nb = 0.5 * (x_prev + x_next)
    xs = x + mu_ref[...] * (nb - x)
    r = xs[:, 0:512]
    k = xs[:, 512:1024]
    v = xs[:, 1024:1536]
    lt = xs[:, 1536:1664]
    a = _sigmoid(a0_ref[...] + _dot(lt.astype(BF16), aup_ref[...]))
    tw = jnp.tanh(lt).astype(BF16)
    c = math.exp(-0.5)
    l0 = -c * _sigmoid(w00_ref[...] + _dot(tw, wup0_ref[...]))
    l1 = -c * _sigmoid(w01_ref[...] + _dot(tw, wup1_ref[...]))
    kk = k * kk_ref[...]
    s1, s2 = _split2(kk * kk)
    ss = _dot(s1, ones_ref[...]) + _dot(s2, ones_ref[...])
    kk = kk / jnp.maximum(jnp.sqrt(ss), 1e-12)
    r_o[0] = r.astype(BF16)
    k_o[0] = (k * (1.0 + (a - 1.0) * ka_ref[...])).astype(BF16)
    v_o[0] = v.astype(BF16)
    kk_o[0] = kk.astype(BF16)
    b_o[0] = (kk * a).astype(BF16)
    l0_o[0] = l0
    l1_o[0] = l1


def rw_prep(h_rw, p, tile=256):
    b, l, w = h_rw.shape
    nt = l // tile
    tb = tile // 8
    tok = lambda bi, i: (bi, i, 0)
    c2 = lambda bi, i: (0, 0)
    params = [p['mu'], p['aup'], p['wup0'], p['wup1'], p['a0'], p['w00'], p['w01'],
              p['kk'], p['ka'], p['ones']]
    in_specs = [pl.BlockSpec((1, tile, w), tok),
                pl.BlockSpec((1, 8, w), lambda bi, i: (bi, jnp.maximum(i * tb - 1, 0), 0)),
                pl.BlockSpec((1, 8, w), lambda bi, i: (bi, jnp.minimum((i + 1) * tb, nt * tb - 1), 0))]
    in_specs += [pl.BlockSpec(a.shape, c2) for a in params]
    return pl.pallas_call(
        functools.partial(_rw_prep_kernel, tile=tile),
        out_shape=[jax.ShapeDtypeStruct((b, l, RW_WIDTH), d) for d in (BF16,) * 5 + (F32,) * 2],
        grid=(b, nt),
        in_specs=in_specs,
        out_specs=[pl.BlockSpec((1, tile, RW_WIDTH), tok)] * 7,
        compiler_params=pltpu.CompilerParams(dimension_semantics=("parallel", "parallel"),
                                             vmem_limit_bytes=VMEM_LIMIT),
        name="rw_prep",
    )(h_rw, h_rw, h_rw, *params)


def _rw_block_diag(x, bmask):
    xb = x.astype(BF16)
    return jnp.where(bmask, jnp.concatenate([xb, xb, xb, xb], axis=0), jnp.zeros((), BF16))


def _rw_chunk_prepare(r, k, v, kk, b, lw, masks, reverse):
    tri, strict, incl, bmask, eye = masks
    c = RW_CHUNK
    bd = functools.partial(_rw_block_diag, bmask=bmask)

    l1, l2, l3 = _split3(lw)
    cl = _dot(tri, l1) + _dot(tri, l2) + _dot(tri, l3)
    yield
    cl_end = cl[0:1, :] if reverse else cl[c - 1:c, :]
    e_neg = jnp.exp(-cl)
    e_end = jnp.exp(cl_end - cl)
    kkt = kk * jnp.exp(cl - lw)
    rt = r * jnp.exp(cl)
    lhs = jnp.concatenate([kkt, rt], axis=0).astype(BF16)
    ab = _dot_nt(lhs, bd(b * e_neg))
    ak = _dot_nt(lhs, bd(k * e_neg))
    yield
    a_bb = jnp.where(strict, ab[:c], 0.0)
    a_rb = jnp.where(incl, ab[c:], 0.0).astype(BF16)
    a_bk = jnp.where(strict, ak[:c], 0.0).astype(BF16)
    a_rk = jnp.where(incl, ak[c:], 0.0).astype(BF16)
    bdv = bd(v)

    pw = -a_bb
    t = eye + pw
    pw = _dot(pw.astype(BF16), bd(pw))
    avk = _dot(jnp.concatenate([a_bk, a_rk], axis=0), bdv)
    av = avk[:c]
    yield
    for _ in range(4):
        res = _dot(jnp.concatenate([pw, t], axis=0).astype(BF16), bd(pw))
        yield
        pw = res[:c]
        t = t + res[c:]
    inv = (t + _dot(t.astype(BF16), bd(pw))).astype(BF16)
    yield
    un = -_dot(inv, jnp.concatenate([bd(av), bd(kkt)], axis=1))
    yield
    u0 = un[:, :RW_GROUP]
    ng = un[:, RW_GROUP:].astype(BF16)
    bde = (b * e_end).astype(BF16)
    ar = _dot(a_rb, jnp.concatenate([bd(u0), bd(ng)], axis=1))
    y0 = ar[:, :RW_GROUP] + avk[c:]
    rq = (rt + ar[:, RW_GROUP:]).astype(BF16)
    trans = _dot_tn(ng, bde)
    drive = _dot_tn(jnp.concatenate([u0.astype(BF16), v.astype(BF16)], axis=0),
                    jnp.concatenate([bde, (k * e_end).astype(BF16)], axis=0))
    yield
    trans = jnp.where(bmask, trans, 0.0).astype(BF16)
    drive = jnp.where(bmask, drive, 0.0)
    return rq, y0, trans, drive, jnp.exp(cl_end)


def _rw_apply(tiles, preps, s_ref, o_ref):
    for (rows, g, lanes), (rq, y0, trans, drive, p_end) in zip(tiles, preps):
        s0 = s_ref[g]
        s0b = s0.astype(BF16)
        s_ref[g] = s0 * p_end + _dot(s0b, trans) + drive
        o_ref[0, rows, lanes] = y0 + _dot_nt(rq, s0b)
        if g == RW_WIDTH // RW_GROUP - 1:
            yield


def _rw_masks(reverse):
    c, g = RW_CHUNK, RW_GROUP
    r2 = lax.broadcasted_iota(jnp.int32, (c, c), 0)
    c2 = lax.broadcasted_iota(jnp.int32, (c, c), 1)
    tri = jnp.where((c2 >= r2) if reverse else (c2 <= r2), 1.0, 0.0).astype(BF16)
    t = lax.broadcasted_iota(jnp.int32, (c, g), 0)
    s = lax.broadcasted_iota(jnp.int32, (c, g), 1) & (c - 1)
    strict = (s > t) if reverse else (s < t)
    incl = (s >= t) if reverse else (s <= t)
    eye = jnp.where(s == t, 1.0, 0.0).astype(F32)
    bi = lax.broadcasted_iota(jnp.int32, (g, g), 0) >> 6
    bj = lax.broadcasted_iota(jnp.int32, (g, g), 1) >> 6
    return tri, strict, incl, bi == bj, eye


def _rw_scan_kernel(*refs, reverse, final):
    if final:
        (r_ref, k_ref, v_ref, kk_ref, b_ref, lw_ref, yprev_ref, z_ref,
         rk_ref, lng_ref, lnb_ref, ones_ref, o_ref, s_ref) = refs
    else:
        (r_ref, k_ref, v_ref, kk_ref, b_ref, lw_ref, o_ref, s_ref) = refs

    @pl.when(pl.program_id(1) == 0)
    def _():
        s_ref[...] = jnp.zeros_like(s_ref)

    masks = _rw_masks(reverse)
    n_chunks = RW_TILE // RW_CHUNK
    order = list(range(n_chunks - 1, -1, -1) if reverse else range(n_chunks))
    n_groups = RW_WIDTH // RW_GROUP
    pending = None
    for w in range(0, n_chunks, RW_WAVE):
        tiles = [(slice(ci * RW_CHUNK, (ci + 1) * RW_CHUNK), g, slice(g * RW_GROUP, (g + 1) * RW_GROUP))
                 for ci in order[w:w + RW_WAVE] for g in range(n_groups)]
        load = lambda ref, rows, lanes: ref[0, rows, lanes].astype(F32)
        gens = [_rw_chunk_prepare(load(r_ref, rows, lanes), load(k_ref, rows, lanes),
                                  load(v_ref, rows, lanes), load(kk_ref, rows, lanes),
                                  load(b_ref, rows, lanes), lw_ref[0, rows, lanes], masks, reverse)
                for rows, g, lanes in tiles]
        if pending is not None:
            gens.append(_rw_apply(*pending, s_ref, o_ref))
        preps = _lockstep(gens)[:len(tiles)]
        pending = (tiles, preps)
    _lockstep([_rw_apply(*pending, s_ref, o_ref)])

    if not final:
        return
    ones = ones_ref[...]
    inv_n = 1.0 / RW_HEAD

    def head_sum(x):
        return _dot(x.astype(BF16), ones)

    y = o_ref[0] + yprev_ref[0]
    mean = head_sum(y) * inv_n
    d = y - mean
    var = head_sum(d * d) * inv_n
    yn = d * lax.rsqrt(var + RW_LN_EPS) * lng_ref[...] + lnb_ref[...]
    bonus = head_sum(r_ref[0].astype(F32) * k_ref[0].astype(F32) * rk_ref[...]) * v_ref[0].astype(F32)
    o_ref[0] = (yn + bonus) * _silu(z_ref[0])


def rw_sweep(seqs, reverse, final_args=None):
    b, l, _ = seqs[0].shape
    nt = l // RW_TILE
    final = final_args is not None
    if reverse:
        tok = lambda bi, i: (bi, nt - 1 - i, 0)
    else:
        tok = lambda bi, i: (bi, i, 0)
    c2 = lambda bi, i: (0, 0)
    tok_spec = pl.BlockSpec((1, RW_TILE, RW_WIDTH), tok)
    in_specs = [tok_spec] * 6
    args = list(seqs)
    if final:
        yprev, z, rk, lng, lnb, ones = final_args
        in_specs += [tok_spec, tok_spec]
        in_specs += [pl.BlockSpec(a.shape, c2) for a in (rk, lng, lnb, ones)]
        args += [yprev, z, rk, lng, lnb, ones]
    return pl.pallas_call(
        functools.partial(_rw_scan_kernel, reverse=reverse, final=final),
        out_shape=jax.ShapeDtypeStruct((b, l, RW_WIDTH), F32),
        grid=(b, nt),
        in_specs=in_specs,
        out_specs=tok_spec,
        scratch_shapes=[pltpu.VMEM((RW_WIDTH // RW_GROUP, RW_GROUP, RW_GROUP), F32)],
        compiler_params=pltpu.CompilerParams(dimension_semantics=("arbitrary", "arbitrary"),
                                             vmem_limit_bytes=VMEM_LIMIT),
        name="rw_final" if final else "rw_sweep",
    )(*args)


AT_HEAD_ORDER = tuple(8 * p + 4 * par + j for p in range(2) for j in range(4) for par in range(2))
LOG2E = math.log2(math.e)


def _head_rms(t, ones_ref, expand_ref, gain_ref):
    ss = _dot((t * t).astype(BF16), ones_ref[...])
    r1, r2 = _split2(lax.rsqrt(ss * (1.0 / AT_HEAD_DIM) + EPS))
    rb = _dot(r1, expand_ref[...]) + _dot(r2, expand_ref[...])
    return (t * rb * gain_ref[...]).astype(BF16)


def _proj_in_attn_kernel(x_ref, g_ref, w_ref, oq_ref, eq_ref, gq_ref, ok_ref, ek_ref, gk_ref,
                         q_o, k_o, v_o, z_o):
    x = x_ref[...]
    ms = jnp.mean(x * x, axis=-1, keepdims=True)
    h = (x * lax.rsqrt(ms + EPS) * g_ref[...]).astype(BF16)
    c0, c1, c2 = AT_WIDTH, AT_WIDTH + AT_KV_WIDTH, AT_WIDTH + 2 * AT_KV_WIDTH
    q_o[...] = _head_rms(_dot(h, w_ref[:, :c0]), oq_ref, eq_ref, gq_ref)
    k_o[...] = _head_rms(_dot(h, w_ref[:, c0:c1]), ok_ref, ek_ref, gk_ref)
    v_o[...] = _dot(h, w_ref[:, c1:c2]).astype(BF16)
    z_o[...] = _dot(h, w_ref[:, c2:])


def proj_in_attn(x2d, p):
    m = x2d.shape[0]
    n = p['w_in'].shape[1]
    consts = [p['ones_q'], p['exp_q'], p['gain_q'], p['ones_k'], p['exp_k'], p['gain_k']]
    widths = (AT_WIDTH, AT_KV_WIDTH, AT_KV_WIDTH, AT_WIDTH)
    dtypes = (BF16, BF16, BF16, F32)
    return pl.pallas_call(
        _proj_in_attn_kernel,
        out_shape=[jax.ShapeDtypeStruct((m, w), d) for w, d in zip(widths, dtypes)],
        grid=(m // TOK_TILE,),
        in_specs=[pl.BlockSpec((TOK_TILE, D_MODEL), lambda i: (i, 0)),
                  pl.BlockSpec((1, D_MODEL), lambda i: (0, 0)),
                  pl.BlockSpec((D_MODEL, n), lambda i: (0, 0))]
                 + [pl.BlockSpec(c.shape, lambda i: (0, 0)) for c in consts],
        out_specs=[pl.BlockSpec((TOK_TILE, w), lambda i: (i, 0)) for w in widths],
        compiler_params=pltpu.CompilerParams(dimension_semantics=("parallel",),
                                             vmem_limit_bytes=VMEM_LIMIT),
        name="proj_in_attn",
    )(x2d, p['norm'].reshape(1, D_MODEL), p['w_in'], *consts)


def _attn_kernel(sink_ref, q_ref, kp_ref, kc_ref, kn_ref, vp_ref, vc_ref, vn_ref, z_ref, o_ref):
    i = pl.program_id(1)
    nb = pl.num_programs(1)
    blk = AT_BLOCK
    iq = lax.broadcasted_iota(jnp.int32, (blk, 6 * blk), 0)
    jk = lax.broadcasted_iota(jnp.int32, (blk, 6 * blk), 1)
    jk = jnp.where(jk >= 3 * blk, jk - 3 * blk, jk)
    absrel = jnp.abs(blk + iq - jk)
    key_lo = jnp.where(i > 0, 0, blk)
    key_hi = jnp.where(i < nb - 1, 3 * blk, 2 * blk)
    valid = (absrel <= blk) & (jk >= key_lo) & (jk < key_hi)
    absrel_f = absrel.astype(F32)
    odd_col = lax.broadcasted_iota(jnp.int32, (blk, 6 * blk), 1) >= 3 * blk
    lane = lax.broadcasted_iota(jnp.int32, (3 * blk, 2 * AT_HEAD_DIM), 1)
    even_lane = lane < AT_HEAD_DIM
    zero = jnp.zeros((), BF16)

    def tile_chain(tile, k2, v2):
        cols = slice(tile * 128, (tile + 1) * 128)
        h_even, h_odd = AT_HEAD_ORDER[2 * tile], AT_HEAD_ORDER[2 * tile + 1]
        s = _dot_nt(q_ref[0][:, cols], k2)
        yield
        slope_e = LOG2E * 2.0 ** (-8.0 * (h_even + 1) / AT_HEADS)
        slope_o = LOG2E * 2.0 ** (-8.0 * (h_odd + 1) / AT_HEADS)
        s = jnp.where(valid, s - jnp.where(odd_col, slope_o, slope_e) * absrel_f, NEG_INF)
        se, so = s[:, :3 * blk], s[:, 3 * blk:]
        sink_e = sink_ref[h_even] * LOG2E
        sink_o = sink_ref[h_odd] * LOG2E
        me = jnp.maximum(jnp.max(se, axis=-1, keepdims=True), sink_e)
        mo = jnp.maximum(jnp.max(so, axis=-1, keepdims=True), sink_o)
        yield
        pe = jnp.exp2(se - me)
        po = jnp.exp2(so - mo)
        de = jnp.sum(pe, axis=-1, keepdims=True) + jnp.exp2(sink_e - me)
        do = jnp.sum(po, axis=-1, keepdims=True) + jnp.exp2(sink_o - mo)
        yield
        pcat = jnp.concatenate([(pe * (1.0 / de)).astype(BF16), (po * (1.0 / do)).astype(BF16)], axis=1)
        o = _dot(pcat, v2)
        yield
        o_ref[0, :, cols] = o * _silu(z_ref[0, :, cols])

    chains = []
    for p in range(2):
        lanes = slice(p * 128, (p + 1) * 128)
        kt = jnp.concatenate([kp_ref[0][:, lanes], kc_ref[0][:, lanes], kn_ref[0][:, lanes]], axis=0)
        vt = jnp.concatenate([vp_ref[0][:, lanes], vc_ref[0][:, lanes], vn_ref[0][:, lanes]], axis=0)
        k2 = jnp.concatenate([jnp.where(even_lane, kt, zero), jnp.where(even_lane, zero, kt)], axis=0)
        v2 = jnp.concatenate([jnp.where(even_lane, vt, zero), jnp.where(even_lane, zero, vt)], axis=0)
        chains += [tile_chain(4 * p + j, k2, v2) for j in range(4)]
    _lockstep(chains)


def attention(q, k, v, z, sink):
    b, l, _ = q.shape
    nb = l // AT_BLOCK
    cur = lambda bi, i, s: (bi, i, 0)
    prv = lambda bi, i, s: (bi, jnp.maximum(i - 1, 0), 0)
    nxt = lambda bi, i, s: (bi, jnp.minimum(i + 1, nb - 1), 0)
    kv = lambda f: pl.BlockSpec((1, AT_BLOCK, AT_KV_WIDTH), f)
    wide = pl.BlockSpec((1, AT_BLOCK, AT_WIDTH), cur)
    grid_spec = pltpu.PrefetchScalarGridSpec(
        num_scalar_prefetch=1,
        grid=(b, nb),
        in_specs=[wide, kv(prv), kv(cur), kv(nxt), kv(prv), kv(cur), kv(nxt), wide],
        out_specs=wide,
    )
    return pl.pallas_call(
        _attn_kernel,
        out_shape=jax.ShapeDtypeStruct((b, l, AT_WIDTH), F32),
        grid_spec=grid_spec,
        compiler_params=pltpu.CompilerParams(dimension_semantics=("parallel", "parallel"),
                                             vmem_limit_bytes=VMEM_LIMIT),
        name="attn",
    )(sink, q, k, k, k, v, v, v, z)


def _odd_params(j, od_norm, od_w_in, at_q_norm, at_k_norm, at_sink, od_w_out):
    cols = jnp.concatenate([jnp.arange(AT_HEAD_DIM) + h * AT_HEAD_DIM for h in AT_HEAD_ORDER])
    w = od_w_in[j]
    c0, c2 = AT_WIDTH, AT_WIDTH + 2 * AT_KV_WIDTH
    w_in = jnp.concatenate([w[:, :c0][:, cols], w[:, c0:c2], w[:, c2:][:, cols]], axis=1)

    def head_maps(width):
        head = jnp.arange(width) // AT_HEAD_DIM
        ones = (head[:, None] == jnp.arange(128)[None, :]).astype(BF16)
        return ones, ones.T

    ones_q, exp_q = head_maps(AT_WIDTH)
    ones_k, exp_k = head_maps(AT_KV_WIDTH)
    qscale = (AT_HEAD_DIM ** -0.5) * LOG2E
    return {'norm': od_norm[j], 'w_in': w_in.astype(BF16),
            'ones_q': ones_q, 'exp_q': exp_q, 'ones_k': ones_k, 'exp_k': exp_k,
            'gain_q': (jnp.tile(at_q_norm[j].astype(F32), AT_HEADS) * qscale).reshape(1, -1),
            'gain_k': jnp.tile(at_k_norm[j].astype(F32), AT_KV_HEADS).reshape(1, -1),
            'sink': at_sink[j].astype(F32), 'w_out': od_w_out[j][cols, :].astype(BF16)}


def _block_ones(width, block):
    i = jnp.arange(width) // block
    return (i[:, None] == i[None, :]).astype(BF16)


def _even_params(j, ev):
    g = lambda name: ev[name][j]
    row = lambda a: a.astype(F32).reshape(1, -1)
    p = {}
    p['norm'] = g('norm')
    p['w_in'] = g('w_in').astype(BF16)
    p['w_out_a'] = g('w_out')[:S5_WIDTH].astype(BF16)
    p['w_out_b'] = g('w_out')[S5_WIDTH:].astype(BF16)
    p['s5'] = [
        _s5_tables(g('s5_a_re')[d], g('s5_a_im')[d], g('s5_log_dt')[d], g('s5_b_re')[d],
                   g('s5_b_im')[d], g('s5_c_re')[d], g('s5_c_im')[d], reverse=bool(d))
        for d in range(2)]
    p['s5_d'] = row(g('s5_d'))
    p['glu_w'] = g('s5_glu_w').astype(BF16)
    p['glu_b'] = row(g('s5_glu_b'))
    zeros64 = jnp.zeros((64, RW_WIDTH), F32)
    p['mu'] = row(g('rw_mu'))
    p['aup'] = jnp.concatenate([zeros64, g('rw_a_up').astype(F32)], axis=0).astype(BF16)
    p['wup0'] = jnp.concatenate([g('rw_w_up')[0].astype(F32), zeros64], axis=0).astype(BF16)
    p['wup1'] = jnp.concatenate([g('rw_w_up')[1].astype(F32), zeros64], axis=0).astype(BF16)
    p['a0'] = row(g('rw_a0'))
    p['w00'] = row(g('rw_w0')[0])
    p['w01'] = row(g('rw_w0')[1])
    p['kk'] = row(g('rw_k_k'))
    p['ka'] = row(g('rw_k_a'))
    p['ones'] = _block_ones(RW_WIDTH, RW_HEAD)
    p['rk'] = row(g('rw_r_k'))
    p['lng'] = row(g('rw_ln_g'))
    p['lnb'] = row(g('rw_ln_b'))
    return p


def _even_layer(x, p):
    b, l, _ = x.shape
    x2 = x.reshape(b * l, D_MODEL)
    u, z_s5, h_rw, z_rw = proj_in(x2, p['norm'], p['w_in'],
                                  (S5_WIDTH, S5_WIDTH, RW_SHIFTED, RW_WIDTH))
    u = u.reshape(b, l, -1)
    z_s5 = z_s5.reshape(b, l, -1)
    h_rw = h_rw.reshape(b, l, -1)
    z_rw = z_rw.reshape(b, l, -1)

    y_rev = s5_sweep(u, p['s5'][1], reverse=True)
    ya = s5_sweep(u, p['s5'][0], reverse=False,
                  final_args=(y_rev, z_s5, p['s5_d'], p['glu_w'], p['glu_b']))

    r, k, v, kk, bb, l0, l1 = rw_prep(h_rw, p)
    yb_rev = rw_sweep((r, k, v, kk, bb, l1), reverse=True)
    yb = rw_sweep((r, k, v, kk, bb, l0), reverse=False,
                  final_args=(yb_rev, z_rw, p['rk'], p['lng'], p['lnb'], p['ones']))

    out = proj_out(x2, [ya.reshape(b * l, -1), yb.reshape(b * l, -1)],
                   [p['w_out_a'], p['w_out_b']])
    return out.reshape(b, l, D_MODEL)


def _odd_layer(x, p):
    b, l, _ = x.shape
    x2 = x.reshape(b * l, D_MODEL)
    q, k, v, z = proj_in_attn(x2, p)
    sh = lambda t: t.reshape(b, l, -1)
    o = attention(sh(q), sh(k), sh(v), sh(z), p['sink'])
    out = proj_out(x2, [o.reshape(b * l, -1)], [p['w_out']])
    return out.reshape(b, l, D_MODEL)


def kernel(x_prompt, x_sample, ev_norm, ev_w_in, s5_a_re, s5_a_im, s5_log_dt, s5_b_re, s5_b_im, s5_c_re, s5_c_im, s5_d, s5_glu_w, s5_glu_b, rw_mu, rw_w0, rw_w_up, rw_a0, rw_a_up, rw_k_k, rw_k_a, rw_r_k, rw_ln_g, rw_ln_b, ev_w_out, od_norm, od_w_in, at_q_norm, at_k_norm, at_sink, od_w_out):
    ev = {'norm': ev_norm, 'w_in': ev_w_in, 's5_a_re': s5_a_re, 's5_a_im': s5_a_im,
          's5_log_dt': s5_log_dt, 's5_b_re': s5_b_re, 's5_b_im': s5_b_im,
          's5_c_re': s5_c_re, 's5_c_im': s5_c_im, 's5_d': s5_d, 's5_glu_w': s5_glu_w,
          's5_glu_b': s5_glu_b, 'rw_mu': rw_mu, 'rw_w0': rw_w0, 'rw_w_up': rw_w_up,
          'rw_a0': rw_a0, 'rw_a_up': rw_a_up, 'rw_k_k': rw_k_k, 'rw_k_a': rw_k_a,
          'rw_r_k': rw_r_k, 'rw_ln_g': rw_ln_g, 'rw_ln_b': rw_ln_b, 'w_out': ev_w_out}
    n_even = ev_norm.shape[0]
    n_odd = od_norm.shape[0]
    even = [_even_params(j, ev) for j in range(n_even)]
    odd = [_odd_params(j, od_norm, od_w_in, at_q_norm, at_k_norm, at_sink, od_w_out)
           for j in range(n_odd)]

    def trunk(x):
        for layer in range(n_even + n_odd):
            j = layer // 2
            x = _even_layer(x, even[j]) if layer % 2 == 0 else _odd_layer(x, odd[j])
        return x

    return (trunk(x_prompt), trunk(x_sample))
```

```python
import functools
import math

import jax
import jax.numpy as jnp
from jax import lax
from jax.experimental import pallas as pl
from jax.experimental.pallas import tpu as pltpu

F32 = jnp.float32
BF16 = jnp.bfloat16

D_MODEL = 1024
EPS = 1e-6

S5_WIDTH = 512
S5_GROUP = 16
S5_GROUPS = 32
S5_STATE = 64
S5_LANES = S5_GROUPS * S5_STATE
S5_SUB = 16
S5_MID = 7.5
S5_TILE = 512
S5_PART = 128

RW_WIDTH = 512
RW_HEAD = 64
RW_SHIFTED = 3 * RW_WIDTH + 128
RW_LN_EPS = 64e-5
RW_CHUNK = 64
RW_TILE = 512
RW_WAVE = 4
RW_GROUP = 256
RW_HALO = 16

AT_HEADS = 16
AT_KV_HEADS = 4
AT_GROUP = 4
AT_HEAD_DIM = 64
AT_BLOCK = 128
AT_WIDTH = 1024
AT_KV_WIDTH = 256
NEG_INF = -1e30

TOK_TILE = 512
VMEM_LIMIT = 48 * 1024 * 1024


def _dot(a, b):
    return jnp.dot(a, b, preferred_element_type=F32)


def _dot_nt(a, b):
    return lax.dot_general(a, b, (((1,), (1,)), ((), ())), preferred_element_type=F32)


def _dot_tn(a, b):
    return lax.dot_general(a, b, (((0,), (0,)), ((), ())), preferred_element_type=F32)


def _lockstep(gens):
    gens = list(gens)
    results = [None] * len(gens)
    live = list(range(len(gens)))
    while live:
        still = []
        for i in live:
            try:
                next(gens[i])
                still.append(i)
            except StopIteration as done:
                results[i] = done.value
        live = still
    return results


def _split3(x):
    x1 = x.astype(BF16)
    r1 = x - x1.astype(F32)
    x2 = r1.astype(BF16)
    x3 = (r1 - x2.astype(F32)).astype(BF16)
    return x1, x2, x3


def _split2(x):
    x1 = x.astype(BF16)
    x2 = (x - x1.astype(F32)).astype(BF16)
    return x1, x2


def _sigmoid(x):
    return 1.0 / (1.0 + jnp.exp(-x))


def _silu(x):
    return x * _sigmoid(x)


def _proj_in_kernel(x_ref, g_ref, w_ref, *out_refs, splits):
    x = x_ref[...]
    ms = jnp.mean(x * x, axis=-1, keepdims=True)
    h = (x * lax.rsqrt(ms + EPS) * g_ref[...]).astype(BF16)
    off = 0
    for o_ref, n in zip(out_refs, splits):
        o_ref[...] = _dot(h, w_ref[:, off:off + n]).astype(o_ref.dtype)
        off += n


def proj_in(x2d, g, w_bf16, splits):
    m = x2d.shape[0]
    n = w_bf16.shape[1]
    assert sum(splits) == n and m % TOK_TILE == 0
    return pl.pallas_call(
        functools.partial(_proj_in_kernel, splits=splits),
        out_shape=[jax.ShapeDtypeStruct((m, s), BF16) for s in splits],
        grid=(m // TOK_TILE,),
        in_specs=[pl.BlockSpec((TOK_TILE, D_MODEL), lambda i: (i, 0)),
                  pl.BlockSpec((1, D_MODEL), lambda i: (0, 0)),
                  pl.BlockSpec((D_MODEL, n), lambda i: (0, 0))],
        out_specs=[pl.BlockSpec((TOK_TILE, s), lambda i: (i, 0)) for s in splits],
        compiler_params=pltpu.CompilerParams(dimension_semantics=("parallel",),
                                             vmem_limit_bytes=VMEM_LIMIT),
        name="proj_in",
    )(x2d, g.reshape(1, D_MODEL), w_bf16)


def _proj_out_kernel(*refs, n_in):
    x_ref = refs[0]
    y_refs = refs[1:1 + n_in]
    w_refs = refs[1 + n_in:1 + 2 * n_in]
    o_ref = refs[1 + 2 * n_in]
    acc = x_ref[...]
    for y_ref, w_ref in zip(y_refs, w_refs):
        acc = acc + _dot(y_ref[...].astype(BF16), w_ref[...])
    o_ref[...] = acc


def proj_out(x2d, ys, ws_bf16):
    m = x2d.shape[0]
    n_in = len(ys)
    in_specs = [pl.BlockSpec((TOK_TILE, D_MODEL), lambda i: (i, 0))]
    in_specs += [pl.BlockSpec((TOK_TILE, y.shape[1]), lambda i: (i, 0)) for y in ys]
    in_specs += [pl.BlockSpec(w.shape, lambda i: (0, 0)) for w in ws_bf16]
    return pl.pallas_call(
        functools.partial(_proj_out_kernel, n_in=n_in),
        out_shape=jax.ShapeDtypeStruct((m, D_MODEL), F32),
        grid=(m // TOK_TILE,),
        in_specs=in_specs,
        out_specs=pl.BlockSpec((TOK_TILE, D_MODEL), lambda i: (i, 0)),
        compiler_params=pltpu.CompilerParams(dimension_semantics=("parallel",),
                                             vmem_limit_bytes=VMEM_LIMIT),
        name="proj_out",
    )(x2d, *ys, *ws_bf16)


def _s5_kernel(*refs, reverse, final):
    if final:
        (u_ref, wb_ref, wc_ref, ein_r, ein_i, eout_r, eout_i, cin_r, cin_i,
         yprev_ref, z_ref, d_ref, gw_ref, gb_ref,
         o_ref, hin_r, hin_i, h_r, h_i, car_r, car_i, y_ref) = refs
    else:
        (u_ref, wb_ref, wc_ref, ein_r, ein_i, eout_r, eout_i, cin_r, cin_i,
         o_ref, hin_r, hin_i, h_r, h_i, car_r, car_i, y_ref) = refs

    @pl.when(pl.program_id(1) == 0)
    def _():
        car_r[...] = jnp.zeros_like(car_r)
        car_i[...] = jnp.zeros_like(car_i)

    row = lax.broadcasted_iota(jnp.int32, (S5_SUB, S5_SUB), 0)
    col = lax.broadcasted_iota(jnp.int32, (S5_SUB, S5_SUB), 1)
    tri = jnp.where((col >= row) if reverse else (col <= row), 1.0, 0.0).astype(BF16)
    last = 0 if reverse else S5_SUB - 1
    n_part = S5_TILE // S5_PART
    n_sub = S5_PART // S5_SUB
    carry = [car_r[...], car_i[...]]

    def part_rows(n):
        n = (n_part - 1 - n) if reverse else n
        return n * S5_PART

    def project_in(n):
        r0 = part_rows(n)
        ub16 = u_ref[0, r0:r0 + S5_PART, :]
        for ub in range(4):
            bu = _dot(ub16[:, ub * 128:(ub + 1) * 128], wb_ref[ub])
            hin_r[r0:r0 + S5_PART, ub * 512:(ub + 1) * 512] = bu[:, :512]
            hin_i[r0:r0 + S5_PART, ub * 512:(ub + 1) * 512] = bu[:, 512:]
            yield

    def scan(n):
        r0 = part_rows(n)
        cr, ci = carry
        for jj in range(n_sub):
            j = (n_sub - 1 - jj) if reverse else jj
            rows = slice(r0 + j * S5_SUB, r0 + (j + 1) * S5_SUB)
            xr = hin_r[rows, :]
            xi = hin_i[rows, :]
            er = ein_r[...]
            ei = ein_i[...]
            sr = (xr * er - xi * ei).astype(BF16)
            si = (xr * ei + xi * er).astype(BF16)
            add_r = cin_r[...] * cr - cin_i[...] * ci
            add_i = cin_r[...] * ci + cin_i[...] * cr
            cum_r = _dot(tri, sr) + add_r
            cum_i = _dot(tri, si) + add_i
            fr = eout_r[...]
            fi = eout_i[...]
            hr = cum_r * fr - cum_i * fi
            hi = cum_r * fi + cum_i * fr
            h_r[rows, :] = hr
            h_i[rows, :] = hi
            cr = hr[last:last + 1, :]
            ci = hi[last:last + 1, :]
            if jj % 2 == 1:
                yield
        carry[0], carry[1] = cr, ci

    def project_out(n):
        r0 = part_rows(n)
        for ub in range(4):
            hcat = jnp.concatenate([h_r[r0:r0 + S5_PART, ub * 512:(ub + 1) * 512],
                                    h_i[r0:r0 + S5_PART, ub * 512:(ub + 1) * 512]], axis=1)
            y_ref[r0:r0 + S5_PART, ub * 128:(ub + 1) * 128] = _dot(hcat.astype(BF16), wc_ref[ub])
            yield

    for step in range(n_part + 2):
        stage = []
        if step < n_part:
            stage.append(project_in(step))
        if 0 <= step - 1 < n_part:
            stage.append(scan(step - 1))
        if 0 <= step - 2 < n_part:
            stage.append(project_out(step - 2))
        _lockstep(stage)
    car_r[...] = carry[0]
    car_i[...] = carry[1]

    if not final:
        o_ref[0] = y_ref[...].astype(BF16)
        return
    y = y_ref[...] + yprev_ref[0].astype(F32) + d_ref[...] * u_ref[0].astype(F32)
    y = jax.nn.gelu(y)
    gate = _dot(y.astype(BF16), gw_ref[...]) + gb_ref[...]
    y = y * _sigmoid(gate)
    o_ref[0] = (y * _silu(z_ref[0].astype(F32))).astype(BF16)


def s5_sweep(u, tabs, reverse, final_args=None):
    b, l, _ = u.shape
    nt = l // S5_TILE
    final = final_args is not None
    if reverse:
        tok = lambda bi, i: (bi, nt - 1 - i, 0)
    else:
        tok = lambda bi, i: (bi, i, 0)
    c2 = lambda bi, i: (0, 0)
    c3 = lambda bi, i: (0, 0, 0)
    wb, wc, ein_r, ein_i, eout_r, eout_i, cin_r, cin_i = tabs
    in_specs = [pl.BlockSpec((1, S5_TILE, S5_WIDTH), tok),
                pl.BlockSpec(wb.shape, c3), pl.BlockSpec(wc.shape, c3)]
    in_specs += [pl.BlockSpec(t.shape, c2) for t in (ein_r, ein_i, eout_r, eout_i, cin_r, cin_i)]
    args = [u, wb, wc, ein_r, ein_i, eout_r, eout_i, cin_r, cin_i]
    if final:
        yprev, z, d, gw, gb = final_args
        in_specs += [pl.BlockSpec((1, S5_TILE, S5_WIDTH), tok),
                     pl.BlockSpec((1, S5_TILE, S5_WIDTH), tok),
                     pl.BlockSpec(d.shape, c2), pl.BlockSpec(gw.shape, c2),
                     pl.BlockSpec(gb.shape, c2)]
        args += [yprev, z, d, gw, gb]
    lanes = S5_LANES
    return pl.pallas_call(
        functools.partial(_s5_kernel, reverse=reverse, final=final),
        out_shape=jax.ShapeDtypeStruct((b, l, S5_WIDTH), BF16),
        grid=(b, nt),
        in_specs=in_specs,
        out_specs=pl.BlockSpec((1, S5_TILE, S5_WIDTH), tok),
        scratch_shapes=[pltpu.VMEM((S5_TILE, lanes), F32), pltpu.VMEM((S5_TILE, lanes), F32),
                        pltpu.VMEM((S5_TILE, lanes), F32), pltpu.VMEM((S5_TILE, lanes), F32),
                        pltpu.VMEM((1, lanes), F32), pltpu.VMEM((1, lanes), F32),
                        pltpu.VMEM((S5_TILE, S5_WIDTH), F32)],
        compiler_params=pltpu.CompilerParams(dimension_semantics=("arbitrary", "arbitrary"),
                                             vmem_limit_bytes=VMEM_LIMIT),
        name="s5_final" if final else "s5_sweep",
    )(*args)


def _s5_tables(a_re, a_im, log_dt, b_re, b_im, c_re, c_im, reverse):
    g, n = S5_GROUPS, S5_STATE
    dt = jnp.exp(log_dt.astype(F32))[:, None]
    ar = a_re.astype(F32)
    ai = a_im.astype(F32)
    mag = jnp.exp(ar * dt)
    lr = mag * jnp.cos(ai * dt)
    li = mag * jnp.sin(ai * dt)
    den = ar * ar + ai * ai
    nr = lr - 1.0
    qr = (nr * ar + li * ai) / den
    qi = (li * ar - nr * ai) / den
    br = b_re.astype(F32)
    bi = b_im.astype(F32)
    bbr = qr[..., None] * br - qi[..., None] * bi
    bbi = qr[..., None] * bi + qi[..., None] * br
    eye = jnp.eye(8, dtype=F32)

    def in_blocks(t):
        t = t.transpose(0, 2, 1).reshape(4, 8, S5_GROUP, n)
        return jnp.einsum('ugcn,gh->ugchn', t, eye).reshape(4, 128, 512)

    wb = jnp.concatenate([in_blocks(bbr), in_blocks(bbi)], axis=2).astype(BF16)

    def out_blocks(t):
        t = t.astype(F32).transpose(0, 2, 1).reshape(4, 8, n, S5_GROUP)
        return jnp.einsum('ugnc,gh->ugnhc', t, eye).reshape(4, 512, 128)

    wc = jnp.concatenate([out_blocks(c_re), -out_blocks(c_im)], axis=1).astype(BF16)

    la = (ar * dt).reshape(1, -1)
    th = (ai * dt).reshape(1, -1)
    pos = jnp.arange(S5_SUB, dtype=F32)[:, None]

    def powers(e):
        m = jnp.exp(e * la)
        return m * jnp.cos(e * th), m * jnp.sin(e * th)

    if reverse:
        ein = powers(pos - S5_MID)
        eout = powers(S5_MID - pos)
        cin = powers(jnp.full((1, 1), S5_SUB - S5_MID, F32))
    else:
        ein = powers(S5_MID - pos)
        eout = powers(pos - S5_MID)
        cin = powers(jnp.full((1, 1), S5_MID + 1.0, F32))
    return (wb, wc, ein[0], ein[1], eout[0], eout[1], cin[0], cin[1])


def _rw_prep_kernel(x_ref, xp_ref, xn_ref, mu_ref, aup_ref, wup0_ref, wup1_ref,
                    a0_ref, w00_ref, w01_ref, kk_ref, ka_ref, ones_ref,
                    r_o, k_o, v_o, kk_o, b_o, l0_o, l1_o, *, tile):
    i = pl.program_id(1)
    nt = pl.num_programs(1)
    x = x_ref[0].astype(F32)
    rowi = lax.broadcasted_iota(jnp.int32, x.shape, 0)
    halo = RW_HALO
    prev_row = xp_ref[0][halo - 1:halo, :].astype(F32) * jnp.where(i > 0, 1.0, 0.0)
    next_row = xn_ref[0][0:1, :].astype(F32) * jnp.where(i < nt - 1, 1.0, 0.0)
    x_prev = jnp.where(rowi == 0, prev_row, pltpu.roll(x, 1, 0))
    x_next = jnp.where(rowi == tile - 1, next_row, pltpu.roll(x, tile - 1, 0))
    nb = 0.5 * (x_prev + x_next)
    xs = x + mu_ref[...] * (nb - x)
    r = xs[:, 0:512]
    k = xs[:, 512:1024]
    v = xs[:, 1024:1536]
    lt = xs[:, 1536:1664]
    a = _sigmoid(a0_ref[...] + _dot(lt.astype(BF16), aup_ref[...]))
    tw = jnp.tanh(lt).astype(BF16)
    c = math.exp(-0.5)
    l0 = -c * _sigmoid(w00_ref[...] + _dot(tw, wup0_ref[...]))
    l1 = -c * _sigmoid(w01_ref[...] + _dot(tw, wup1_ref[...]))
    kk = k * kk_ref[...]
    s1, s2 = _split2(kk * kk)
    ss = _dot(s1, ones_ref[...]) + _dot(s2, ones_ref[...])
    kk = kk / jnp.maximum(jnp.sqrt(ss), 1e-12)
    r_o[0] = r.astype(BF16)
    k_o[0] = (k * (1.0 + (a - 1.0) * ka_ref[...])).astype(BF16)
    v_o[0] = v.astype(BF16)
    kk_o[0] = kk.astype(BF16)
    b_o[0] = (kk * a).astype(BF16)
    l0_o[0] = l0
    l1_o[0] = l1


def rw_prep(h_rw, p, tile=256):
    b, l, w = h_rw.shape
    nt = l // tile
    tb = tile // RW_HALO
    tok = lambda bi, i: (bi, i, 0)
    c2 = lambda bi, i: (0, 0)
    params = [p['mu'], p['aup'], p['wup0'], p['wup1'], p['a0'], p['w00'], p['w01'],
              p['kk'], p['ka'], p['ones']]
    in_specs = [pl.BlockSpec((1, tile, w), tok),
                pl.BlockSpec((1, RW_HALO, w), lambda bi, i: (bi, jnp.maximum(i * tb - 1, 0), 0)),
                pl.BlockSpec((1, RW_HALO, w),
                             lambda bi, i: (bi, jnp.minimum((i + 1) * tb, nt * tb - 1), 0))]
    in_specs += [pl.BlockSpec(a.shape, c2) for a in params]
    return pl.pallas_call(
        functools.partial(_rw_prep_kernel, tile=tile),
        out_shape=[jax.ShapeDtypeStruct((b, l, RW_WIDTH), d) for d in (BF16,) * 5 + (F32,) * 2],
        grid=(b, nt),
        in_specs=in_specs,
        out_specs=[pl.BlockSpec((1, tile, RW_WIDTH), tok)] * 7,
        compiler_params=pltpu.CompilerParams(dimension_semantics=("parallel", "parallel"),
                                             vmem_limit_bytes=VMEM_LIMIT),
        name="rw_prep",
    )(h_rw, h_rw, h_rw, *params)


def _rw_block_diag(x, bmask):
    xb = x.astype(BF16)
    return jnp.where(bmask, jnp.concatenate([xb, xb, xb, xb], axis=0), jnp.zeros((), BF16))


def _rw_chunk_prepare(r, k, v, kk, b, lw, masks, reverse):
    tri, strict, incl, bmask, eye = masks
    c = RW_CHUNK
    bd = functools.partial(_rw_block_diag, bmask=bmask)

    l1, l2, l3 = _split3(lw)
    cl = _dot(tri, l1) + _dot(tri, l2) + _dot(tri, l3)
    yield
    cl_end = cl[0:1, :] if reverse else cl[c - 1:c, :]
    e_neg = jnp.exp(-cl)
    e_end = jnp.exp(cl_end - cl)
    kkt = kk * jnp.exp(cl - lw)
    rt = r * jnp.exp(cl)
    lhs = jnp.concatenate([kkt, rt], axis=0).astype(BF16)
    ab = _dot_nt(lhs, bd(b * e_neg))
    ak = _dot_nt(lhs, bd(k * e_neg))
    yield
    a_bb = jnp.where(strict, ab[:c], 0.0)
    a_rb = jnp.where(incl, ab[c:], 0.0).astype(BF16)
    a_bk = jnp.where(strict, ak[:c], 0.0).astype(BF16)
    a_rk = jnp.where(incl, ak[c:], 0.0).astype(BF16)
    bdv = bd(v)

    pw = -a_bb
    t = eye + pw
    pw = _dot(pw.astype(BF16), bd(pw))
    avk = _dot(jnp.concatenate([a_bk, a_rk], axis=0), bdv)
    av = avk[:c]
    yield
    for _ in range(4):
        res = _dot(jnp.concatenate([pw, t], axis=0).astype(BF16), bd(pw))
        yield
        pw = res[:c]
        t = t + res[c:]
    inv = (t + _dot(t.astype(BF16), bd(pw))).astype(BF16)
    yield
    un = -_dot(inv, jnp.concatenate([bd(av), bd(kkt)], axis=1))
    yield
    u0 = un[:, :RW_GROUP]
    ng = un[:, RW_GROUP:].astype(BF16)
    bde = (b * e_end).astype(BF16)
    ar = _dot(a_rb, jnp.concatenate([bd(u0), bd(ng)], axis=1))
    y0 = ar[:, :RW_GROUP] + avk[c:]
    rq = (rt + ar[:, RW_GROUP:]).astype(BF16)
    trans = _dot_tn(ng, bde)
    drive = _dot_tn(jnp.concatenate([u0.astype(BF16), v.astype(BF16)], axis=0),
                    jnp.concatenate([bde, (k * e_end).astype(BF16)], axis=0))
    yield
    trans = jnp.where(bmask, trans, 0.0).astype(BF16)
    drive = jnp.where(bmask, drive, 0.0)
    return rq, y0, trans, drive, jnp.exp(cl_end)


def _rw_apply(tiles, preps, s_ref, y_ref):
    for (rows, g, lanes), (rq, y0, trans, drive, p_end) in zip(tiles, preps):
        s0 = s_ref[g]
        s0b = s0.astype(BF16)
        s_ref[g] = s0 * p_end + _dot(s0b, trans) + drive
        y_ref[rows, lanes] = y0 + _dot_nt(rq, s0b)
        if g == RW_WIDTH // RW_GROUP - 1:
            yield


def _rw_masks(reverse):
    c, g = RW_CHUNK, RW_GROUP
    r2 = lax.broadcasted_iota(jnp.int32, (c, c), 0)
    c2 = lax.broadcasted_iota(jnp.int32, (c, c), 1)
    tri = jnp.where((c2 >= r2) if reverse else (c2 <= r2), 1.0, 0.0).astype(BF16)
    t = lax.broadcasted_iota(jnp.int32, (c, g), 0)
    s = lax.broadcasted_iota(jnp.int32, (c, g), 1) & (c - 1)
    strict = (s > t) if reverse else (s < t)
    incl = (s >= t) if reverse else (s <= t)
    eye = jnp.where(s == t, 1.0, 0.0).astype(F32)
    bi = lax.broadcasted_iota(jnp.int32, (g, g), 0) >> 6
    bj = lax.broadcasted_iota(jnp.int32, (g, g), 1) >> 6
    return tri, strict, incl, bi == bj, eye


def _rw_scan_kernel(*refs, reverse, final):
    if final:
        (r_ref, k_ref, v_ref, kk_ref, b_ref, lw_ref, yprev_ref, z_ref,
         rk_ref, lng_ref, lnb_ref, ones_ref, o_ref, s_ref, y_ref) = refs
    else:
        (r_ref, k_ref, v_ref, kk_ref, b_ref, lw_ref, o_ref, s_ref, y_ref) = refs

    @pl.when(pl.program_id(1) == 0)
    def _():
        s_ref[...] = jnp.zeros_like(s_ref)

    masks = _rw_masks(reverse)
    n_chunks = RW_TILE // RW_CHUNK
    order = list(range(n_chunks - 1, -1, -1) if reverse else range(n_chunks))
    n_groups = RW_WIDTH // RW_GROUP
    pending = None
    for w in range(0, n_chunks, RW_WAVE):
        tiles = [(slice(ci * RW_CHUNK, (ci + 1) * RW_CHUNK), g, slice(g * RW_GROUP, (g + 1) * RW_GROUP))
                 for ci in order[w:w + RW_WAVE] for g in range(n_groups)]
        load = lambda ref, rows, lanes: ref[0, rows, lanes].astype(F32)
        gens = [_rw_chunk_prepare(load(r_ref, rows, lanes), load(k_ref, rows, lanes),
                                  load(v_ref, rows, lanes), load(kk_ref, rows, lanes),
                                  load(b_ref, rows, lanes), lw_ref[0, rows, lanes], masks, reverse)
                for rows, g, lanes in tiles]
        if pending is not None:
            gens.append(_rw_apply(*pending, s_ref, y_ref))
        preps = _lockstep(gens)[:len(tiles)]
        pending = (tiles, preps)
    _lockstep([_rw_apply(*pending, s_ref, y_ref)])

    if not final:
        o_ref[0] = y_ref[...].astype(BF16)
        return
    ones = ones_ref[...]
    inv_n = 1.0 / RW_HEAD

    def head_sum(x):
        return _dot(x.astype(BF16), ones)

    y = y_ref[...] + yprev_ref[0].astype(F32)
    mean = head_sum(y) * inv_n
    d = y - mean
    var = head_sum(d * d) * inv_n
    yn = d * lax.rsqrt(var + RW_LN_EPS) * lng_ref[...] + lnb_ref[...]
    bonus = head_sum(r_ref[0].astype(F32) * k_ref[0].astype(F32) * rk_ref[...]) * v_ref[0].astype(F32)
    o_ref[0] = ((yn + bonus) * _silu(z_ref[0].astype(F32))).astype(BF16)


def rw_sweep(seqs, reverse, final_args=None):
    b, l, _ = seqs[0].shape
    nt = l // RW_TILE
    final = final_args is not None
    if reverse:
        tok = lambda bi, i: (bi, nt - 1 - i, 0)
    else:
        tok = lambda bi, i: (bi, i, 0)
    c2 = lambda bi, i: (0, 0)
    tok_spec = pl.BlockSpec((1, RW_TILE, RW_WIDTH), tok)
    in_specs = [tok_spec] * 6
    args = list(seqs)
    if final:
        yprev, z, rk, lng, lnb, ones = final_args
        in_specs += [tok_spec, tok_spec]
        in_specs += [pl.BlockSpec(a.shape, c2) for a in (rk, lng, lnb, ones)]
        args += [yprev, z, rk, lng, lnb, ones]
    return pl.pallas_call(
        functools.partial(_rw_scan_kernel, reverse=reverse, final=final),
        out_shape=jax.ShapeDtypeStruct((b, l, RW_WIDTH), BF16),
        grid=(b, nt),
        in_specs=in_specs,
        out_specs=tok_spec,
        scratch_shapes=[pltpu.VMEM((RW_WIDTH // RW_GROUP, RW_GROUP, RW_GROUP), F32),
                        pltpu.VMEM((RW_TILE, RW_WIDTH), F32)],
        compiler_params=pltpu.CompilerParams(dimension_semantics=("arbitrary", "arbitrary"),
                                             vmem_limit_bytes=VMEM_LIMIT),
        name="rw_final" if final else "rw_sweep",
    )(*args)


AT_HEAD_ORDER = tuple(8 * p + 4 * par + j for p in range(2) for j in range(4) for par in range(2))
LOG2E = math.log2(math.e)


def _head_rms(t, ones_ref, expand_ref, gain_ref):
    ss = _dot((t * t).astype(BF16), ones_ref[...])
    r1, r2 = _split2(lax.rsqrt(ss * (1.0 / AT_HEAD_DIM) + EPS))
    rb = _dot(r1, expand_ref[...]) + _dot(r2, expand_ref[...])
    return (t * rb * gain_ref[...]).astype(BF16)


def _proj_in_attn_kernel(x_ref, g_ref, w_ref, oq_ref, eq_ref, gq_ref, ok_ref, ek_ref, gk_ref,
                         q_o, k_o, v_o, z_o):
    x = x_ref[...]
    ms = jnp.mean(x * x, axis=-1, keepdims=True)
    h = (x * lax.rsqrt(ms + EPS) * g_ref[...]).astype(BF16)
    c0, c1, c2 = AT_WIDTH, AT_WIDTH + AT_KV_WIDTH, AT_WIDTH + 2 * AT_KV_WIDTH
    q_o[...] = _head_rms(_dot(h, w_ref[:, :c0]), oq_ref, eq_ref, gq_ref)
    k_o[...] = _head_rms(_dot(h, w_ref[:, c0:c1]), ok_ref, ek_ref, gk_ref)
    v_o[...] = _dot(h, w_ref[:, c1:c2]).astype(BF16)
    z_o[...] = _dot(h, w_ref[:, c2:]).astype(BF16)


def proj_in_attn(x2d, p):
    m = x2d.shape[0]
    n = p['w_in'].shape[1]
    consts = [p['ones_q'], p['exp_q'], p['gain_q'], p['ones_k'], p['exp_k'], p['gain_k']]
    widths = (AT_WIDTH, AT_KV_WIDTH, AT_KV_WIDTH, AT_WIDTH)
    dtypes = (BF16, BF16, BF16, BF16)
    return pl.pallas_call(
        _proj_in_attn_kernel,
        out_shape=[jax.ShapeDtypeStruct((m, w), d) for w, d in zip(widths, dtypes)],
        grid=(m // TOK_TILE,),
        in_specs=[pl.BlockSpec((TOK_TILE, D_MODEL), lambda i: (i, 0)),
                  pl.BlockSpec((1, D_MODEL), lambda i: (0, 0)),
                  pl.BlockSpec((D_MODEL, n), lambda i: (0, 0))]
                 + [pl.BlockSpec(c.shape, lambda i: (0, 0)) for c in consts],
        out_specs=[pl.BlockSpec((TOK_TILE, w), lambda i: (i, 0)) for w in widths],
        compiler_params=pltpu.CompilerParams(dimension_semantics=("parallel",),
                                             vmem_limit_bytes=VMEM_LIMIT),
        name="proj_in_attn",
    )(x2d, p['norm'].reshape(1, D_MODEL), p['w_in'], *consts)


def _attn_kernel(sink_ref, q_ref, kp_ref, kc_ref, kn_ref, vp_ref, vc_ref, vn_ref, z_ref, bias_ref,
                 o_ref):
    blk = AT_BLOCK
    lane = lax.broadcasted_iota(jnp.int32, (3 * blk, 2 * AT_HEAD_DIM), 1)
    even_lane = lane < AT_HEAD_DIM
    zero = jnp.zeros((), BF16)

    def tile_chain(tile, k2, v2):
        cols = slice(tile * 128, (tile + 1) * 128)
        h_even, h_odd = AT_HEAD_ORDER[2 * tile], AT_HEAD_ORDER[2 * tile + 1]
        s = _dot_nt(q_ref[0][:, cols], k2)
        yield
        s = s + bias_ref[0, tile]
        se, so = s[:, :3 * blk], s[:, 3 * blk:]
        sink_e = sink_ref[h_even] * LOG2E
        sink_o = sink_ref[h_odd] * LOG2E
        me = jnp.maximum(jnp.max(se, axis=-1, keepdims=True), sink_e)
        mo = jnp.maximum(jnp.max(so, axis=-1, keepdims=True), sink_o)
        yield
        pe = jnp.exp2(se - me)
        po = jnp.exp2(so - mo)
        de = jnp.sum(pe, axis=-1, keepdims=True) + jnp.exp2(sink_e - me)
        do = jnp.sum(po, axis=-1, keepdims=True) + jnp.exp2(sink_o - mo)
        yield
        pcat = jnp.concatenate([(pe * (1.0 / de)).astype(BF16), (po * (1.0 / do)).astype(BF16)], axis=1)
        o = _dot(pcat, v2)
        yield
        o_ref[0, :, cols] = (o * _silu(z_ref[0, :, cols].astype(F32))).astype(o_ref.dtype)

    chains = []
    for p in range(2):
        lanes = slice(p * 128, (p + 1) * 128)
        kt = jnp.concatenate([kp_ref[0][:, lanes], kc_ref[0][:, lanes], kn_ref[0][:, lanes]], axis=0)
        vt = jnp.concatenate([vp_ref[0][:, lanes], vc_ref[0][:, lanes], vn_ref[0][:, lanes]], axis=0)
        k2 = jnp.concatenate([jnp.where(even_lane, kt, zero), jnp.where(even_lane, zero, kt)], axis=0)
        v2 = jnp.concatenate([jnp.where(even_lane, vt, zero), jnp.where(even_lane, zero, vt)], axis=0)
        chains += [tile_chain(4 * p + j, k2, v2) for j in range(4)]
    _lockstep(chains)


def _attn_bias():
    blk = AT_BLOCK
    iq = jnp.arange(blk)[:, None]
    jk = jnp.arange(3 * blk)[None, :]
    absrel = jnp.abs(blk + iq - jk)
    in_win = absrel <= blk
    slopes = LOG2E * jnp.exp2(-8.0 * (jnp.asarray(AT_HEAD_ORDER, F32) + 1.0) / AT_HEADS)
    alibi = -slopes[:, None, None] * absrel.astype(F32)[None]
    variants = []
    for key_lo, key_hi in ((blk, 3 * blk), (0, 3 * blk), (0, 2 * blk)):
        ok = in_win & (jk >= key_lo) & (jk < key_hi)
        b = jnp.where(ok[None], alibi, NEG_INF)
        variants.append(jnp.concatenate([b[0::2], b[1::2]], axis=-1))
    return jnp.stack(variants)


def attention(q, k, v, z, sink):
    b, l, _ = q.shape
    nb = l // AT_BLOCK
    assert nb >= 2
    cur = lambda bi, i, s: (bi, i, 0)
    prv = lambda bi, i, s: (bi, jnp.maximum(i - 1, 0), 0)
    nxt = lambda bi, i, s: (bi, jnp.minimum(i + 1, nb - 1), 0)
    edge = lambda bi, i, s: (jnp.where(i == 0, 0, jnp.where(i == nb - 1, 2, 1)), 0, 0, 0)
    kv = lambda f: pl.BlockSpec((1, AT_BLOCK, AT_KV_WIDTH), f)
    wide = pl.BlockSpec((1, AT_BLOCK, AT_WIDTH), cur)
    grid_spec = pltpu.PrefetchScalarGridSpec(
        num_scalar_prefetch=1,
        grid=(b, nb),
        in_specs=[wide, kv(prv), kv(cur), kv(nxt), kv(prv), kv(cur), kv(nxt), wide,
                  pl.BlockSpec((1, 8, AT_BLOCK, 6 * AT_BLOCK), edge)],
        out_specs=wide,
    )
    return pl.pallas_call(
        _attn_kernel,
        out_shape=jax.ShapeDtypeStruct((b, l, AT_WIDTH), BF16),
        grid_spec=grid_spec,
        compiler_params=pltpu.CompilerParams(dimension_semantics=("parallel", "parallel"),
                                             vmem_limit_bytes=VMEM_LIMIT),
        name="attn",
    )(sink, q, k, k, k, v, v, v, z, _attn_bias())


def _odd_params(j, od_norm, od_w_in, at_q_norm, at_k_norm, at_sink, od_w_out):
    cols = jnp.concatenate([jnp.arange(AT_HEAD_DIM) + h * AT_HEAD_DIM for h in AT_HEAD_ORDER])
    w = od_w_in[j]
    c0, c2 = AT_WIDTH, AT_WIDTH + 2 * AT_KV_WIDTH
    w_in = jnp.concatenate([w[:, :c0][:, cols], w[:, c0:c2], w[:, c2:][:, cols]], axis=1)

    def head_maps(width):
        head = jnp.arange(width) // AT_HEAD_DIM
        ones = (head[:, None] == jnp.arange(128)[None, :]).astype(BF16)
        return ones, ones.T

    ones_q, exp_q = head_maps(AT_WIDTH)
    ones_k, exp_k = head_maps(AT_KV_WIDTH)
    qscale = (AT_HEAD_DIM ** -0.5) * LOG2E
    return {'norm': od_norm[j], 'w_in': w_in.astype(BF16),
            'ones_q': ones_q, 'exp_q': exp_q, 'ones_k': ones_k, 'exp_k': exp_k,
            'gain_q': (jnp.tile(at_q_norm[j].astype(F32), AT_HEADS) * qscale).reshape(1, -1),
            'gain_k': jnp.tile(at_k_norm[j].astype(F32), AT_KV_HEADS).reshape(1, -1),
            'sink': at_sink[j].astype(F32), 'w_out': od_w_out[j][cols, :].astype(BF16)}


def _block_ones(width, block):
    i = jnp.arange(width) // block
    return (i[:, None] == i[None, :]).astype(BF16)


def _even_params(j, ev):
    g = lambda name: ev[name][j]
    row = lambda a: a.astype(F32).reshape(1, -1)
    p = {}
    p['norm'] = g('norm')
    p['w_in'] = g('w_in').astype(BF16)
    p['w_out_a'] = g('w_out')[:S5_WIDTH].astype(BF16)
    p['w_out_b'] = g('w_out')[S5_WIDTH:].astype(BF16)
    p['s5'] = [
        _s5_tables(g('s5_a_re')[d], g('s5_a_im')[d], g('s5_log_dt')[d], g('s5_b_re')[d],
                   g('s5_b_im')[d], g('s5_c_re')[d], g('s5_c_im')[d], reverse=bool(d))
        for d in range(2)]
    p['s5_d'] = row(g('s5_d'))
    p['glu_w'] = g('s5_glu_w').astype(BF16)
    p['glu_b'] = row(g('s5_glu_b'))
    zeros64 = jnp.zeros((64, RW_WIDTH), F32)
    p['mu'] = row(g('rw_mu'))
    p['aup'] = jnp.concatenate([zeros64, g('rw_a_up').astype(F32)], axis=0).astype(BF16)
    p['wup0'] = jnp.concatenate([g('rw_w_up')[0].astype(F32), zeros64], axis=0).astype(BF16)
    p['wup1'] = jnp.concatenate([g('rw_w_up')[1].astype(F32), zeros64], axis=0).astype(BF16)
    p['a0'] = row(g('rw_a0'))
    p['w00'] = row(g('rw_w0')[0])
    p['w01'] = row(g('rw_w0')[1])
    p['kk'] = row(g('rw_k_k'))
    p['ka'] = row(g('rw_k_a'))
    p['ones'] = _block_ones(RW_WIDTH, RW_HEAD)
    p['rk'] = row(g('rw_r_k'))
    p['lng'] = row(g('rw_ln_g'))
    p['lnb'] = row(g('rw_ln_b'))
    return p


def _even_layer(x, p):
    b, l, _ = x.shape
    x2 = x.reshape(b * l, D_MODEL)
    u, z_s5, h_rw, z_rw = proj_in(x2, p['norm'], p['w_in'],
                                  (S5_WIDTH, S5_WIDTH, RW_SHIFTED, RW_WIDTH))
    u = u.reshape(b, l, -1)
    z_s5 = z_s5.reshape(b, l, -1)
    h_rw = h_rw.reshape(b, l, -1)
    z_rw = z_rw.reshape(b, l, -1)

    y_rev = s5_sweep(u, p['s5'][1], reverse=True)
    ya = s5_sweep(u, p['s5'][0], reverse=False,
                  final_args=(y_rev, z_s5, p['s5_d'], p['glu_w'], p['glu_b']))

    r, k, v, kk, bb, l0, l1 = rw_prep(h_rw, p)
    yb_rev = rw_sweep((r, k, v, kk, bb, l1), reverse=True)
    yb = rw_sweep((r, k, v, kk, bb, l0), reverse=False,
                  final_args=(yb_rev, z_rw, p['rk'], p['lng'], p['lnb'], p['ones']))

    out = proj_out(x2, [ya.reshape(b * l, -1), yb.reshape(b * l, -1)],
                   [p['w_out_a'], p['w_out_b']])
    return out.reshape(b, l, D_MODEL)


def _odd_layer(x, p):
    b, l, _ = x.shape
    x2 = x.reshape(b * l, D_MODEL)
    q, k, v, z = proj_in_attn(x2, p)
    sh = lambda t: t.reshape(b, l, -1)
    o = attention(sh(q), sh(k), sh(v), sh(z), p['sink'])
    out = proj_out(x2, [o.reshape(b * l, -1)], [p['w_out']])
    return out.reshape(b, l, D_MODEL)


def kernel(x_prompt, x_sample, ev_norm, ev_w_in, s5_a_re, s5_a_im, s5_log_dt, s5_b_re, s5_b_im, s5_c_re, s5_c_im, s5_d, s5_glu_w, s5_glu_b, rw_mu, rw_w0, rw_w_up, rw_a0, rw_a_up, rw_k_k, rw_k_a, rw_r_k, rw_ln_g, rw_ln_b, ev_w_out, od_norm, od_w_in, at_q_norm, at_k_norm, at_sink, od_w_out):
    ev = {'norm': ev_norm, 'w_in': ev_w_in, 's5_a_re': s5_a_re, 's5_a_im': s5_a_im,
          's5_log_dt': s5_log_dt, 's5_b_re': s5_b_re, 's5_b_im': s5_b_im,
          's5_c_re': s5_c_re, 's5_c_im': s5_c_im, 's5_d': s5_d, 's5_glu_w': s5_glu_w,
          's5_glu_b': s5_glu_b, 'rw_mu': rw_mu, 'rw_w0': rw_w0, 'rw_w_up': rw_w_up,
          'rw_a0': rw_a0, 'rw_a_up': rw_a_up, 'rw_k_k': rw_k_k, 'rw_k_a': rw_k_a,
          'rw_r_k': rw_r_k, 'rw_ln_g': rw_ln_g, 'rw_ln_b': rw_ln_b, 'w_out': ev_w_out}
    n_even = ev_norm.shape[0]
    n_odd = od_norm.shape[0]
    even = [_even_params(j, ev) for j in range(n_even)]
    odd = [_odd_params(j, od_norm, od_w_in, at_q_norm, at_k_norm, at_sink, od_w_out)
           for j in range(n_odd)]

    def trunk(x):
        for layer in range(n_even + n_odd):
            j = layer // 2
            x = _even_layer(x, even[j]) if layer % 2 == 0 else _odd_layer(x, odd[j])
        return x

    return (trunk(x_prompt), trunk(x_sample))
```

```python
import functools
import math

import jax
import jax.numpy as jnp
from jax import lax
from jax.experimental import pallas as pl
from jax.experimental.pallas import tpu as pltpu

F32 = jnp.float32
BF16 = jnp.bfloat16

D_MODEL = 1024
EPS = 1e-6

S5_WIDTH = 512
S5_GROUP = 16
S5_GROUPS = 32
S5_STATE = 64
S5_SUPER = 16
S5_ROWS = 256

RW_WIDTH = 512
RW_HEAD = 64
RW_SHIFTED = 3 * RW_WIDTH + 128
RW_LN_EPS = 64e-5
RW_CHUNK = 64
RW_TILE = 512
RW_WAVE = 4
RW_GROUP = 256
RW_HALO = 16

AT_HEADS = 16
AT_KV_HEADS = 4
AT_GROUP = 4
AT_HEAD_DIM = 64
AT_BLOCK = 128
AT_WIDTH = 1024
AT_KV_WIDTH = 256
NEG_INF = -1e30

TOK_TILE = 512
VMEM_LIMIT = 48 * 1024 * 1024


def _dot(a, b):
    return jnp.dot(a, b, preferred_element_type=F32)


def _dot_nt(a, b):
    return lax.dot_general(a, b, (((1,), (1,)), ((), ())), preferred_element_type=F32)


def _dot_tn(a, b):
    return lax.dot_general(a, b, (((0,), (0,)), ((), ())), preferred_element_type=F32)


def _lockstep(gens):
    gens = list(gens)
    results = [None] * len(gens)
    live = list(range(len(gens)))
    while live:
        still = []
        for i in live:
            try:
                next(gens[i])
                still.append(i)
            except StopIteration as done:
                results[i] = done.value
        live = still
    return results


def _split3(x):
    x1 = x.astype(BF16)
    r1 = x - x1.astype(F32)
    x2 = r1.astype(BF16)
    x3 = (r1 - x2.astype(F32)).astype(BF16)
    return x1, x2, x3


def _split2(x):
    x1 = x.astype(BF16)
    x2 = (x - x1.astype(F32)).astype(BF16)
    return x1, x2


def _sigmoid(x):
    return 1.0 / (1.0 + jnp.exp(-x))


def _silu(x):
    return x * _sigmoid(x)


def _proj_in_kernel(x_ref, g_ref, w_ref, *out_refs, splits):
    x = x_ref[...]
    ms = jnp.mean(x * x, axis=-1, keepdims=True)
    h = (x * lax.rsqrt(ms + EPS) * g_ref[...]).astype(BF16)
    off = 0
    for o_ref, n in zip(out_refs, splits):
        o_ref[...] = _dot(h, w_ref[:, off:off + n]).astype(o_ref.dtype)
        off += n


def proj_in(x2d, g, w_bf16, splits):
    m = x2d.shape[0]
    n = w_bf16.shape[1]
    assert sum(splits) == n and m % TOK_TILE == 0
    return pl.pallas_call(
        functools.partial(_proj_in_kernel, splits=splits),
        out_shape=[jax.ShapeDtypeStruct((m, s), BF16) for s in splits],
        grid=(m // TOK_TILE,),
        in_specs=[pl.BlockSpec((TOK_TILE, D_MODEL), lambda i: (i, 0)),
                  pl.BlockSpec((1, D_MODEL), lambda i: (0, 0)),
                  pl.BlockSpec((D_MODEL, n), lambda i: (0, 0))],
        out_specs=[pl.BlockSpec((TOK_TILE, s), lambda i: (i, 0)) for s in splits],
        compiler_params=pltpu.CompilerParams(dimension_semantics=("parallel",),
                                             vmem_limit_bytes=VMEM_LIMIT),
        name="proj_in",
    )(x2d, g.reshape(1, D_MODEL), w_bf16)


def _proj_out_kernel(*refs, n_in):
    x_ref = refs[0]
    y_refs = refs[1:1 + n_in]
    w_refs = refs[1 + n_in:1 + 2 * n_in]
    o_ref = refs[1 + 2 * n_in]
    acc = x_ref[...]
    for y_ref, w_ref in zip(y_refs, w_refs):
        acc = acc + _dot(y_ref[...].astype(BF16), w_ref[...])
    o_ref[...] = acc


def proj_out(x2d, ys, ws_bf16):
    m = x2d.shape[0]
    n_in = len(ys)
    in_specs = [pl.BlockSpec((TOK_TILE, D_MODEL), lambda i: (i, 0))]
    in_specs += [pl.BlockSpec((TOK_TILE, y.shape[1]), lambda i: (i, 0)) for y in ys]
    in_specs += [pl.BlockSpec(w.shape, lambda i: (0, 0)) for w in ws_bf16]
    return pl.pallas_call(
        functools.partial(_proj_out_kernel, n_in=n_in),
        out_shape=jax.ShapeDtypeStruct((m, D_MODEL), F32),
        grid=(m // TOK_TILE,),
        in_specs=in_specs,
        out_specs=pl.BlockSpec((TOK_TILE, D_MODEL), lambda i: (i, 0)),
        compiler_params=pltpu.CompilerParams(dimension_semantics=("parallel",),
                                             vmem_limit_bytes=VMEM_LIMIT),
        name="proj_out",
    )(x2d, *ys, *ws_bf16)


def _s5_row_tiles(u_ref, ub):
    return jnp.concatenate(
        [u_ref[0, :, t * S5_WIDTH + ub * 128:t * S5_WIDTH + (ub + 1) * 128] for t in range(S5_SUPER)],
        axis=1)


def _s5_toeplitz_kernel(u_ref, w_ref, o_ref, *, ub):
    o_ref[0] = _dot(_s5_row_tiles(u_ref, ub), w_ref[...])


def _s5_states_kernel(u_ref, w_ref, o_ref, *, ub):
    res = _dot(_s5_row_tiles(u_ref, ub), w_ref[...])
    for q in range(4):
        o_ref[0, q] = res[:, q * 512:(q + 1) * 512]


def _s5_carry_kernel(hf_ref, hb_ref, w_ref, o_ref):
    h = jnp.concatenate([hf_ref[0, 0], hf_ref[0, 1], hb_ref[0, 0], hb_ref[0, 1]], axis=1)
    o_ref[0] = _dot(h.astype(BF16), w_ref[...])


def _s5_row_call(kernel, name, u16, w, out_shape, out_block, out_map):
    b, r, _ = u16.shape
    return pl.pallas_call(
        kernel,
        out_shape=jax.ShapeDtypeStruct(out_shape, F32),
        grid=(b, r // S5_ROWS),
        in_specs=[pl.BlockSpec((1, S5_ROWS, S5_SUPER * S5_WIDTH), lambda bi, i: (bi, i, 0)),
                  pl.BlockSpec(w.shape, lambda bi, i: (0, 0))],
        out_specs=pl.BlockSpec(out_block, out_map),
        compiler_params=pltpu.CompilerParams(dimension_semantics=("parallel", "parallel"),
                                             vmem_limit_bytes=VMEM_LIMIT),
        name=name,
    )(u16, w)


def _s5_scan_kernel(*refs, reverse):
    s_refs, (mur_ref, mui_ref), h_refs, (cr_ref, ci_ref) = refs[0:4], refs[4:6], refs[6:10], refs[10:12]

    @pl.when(pl.program_id(1) == 0)
    def _():
        cr_ref[...] = jnp.zeros_like(cr_ref)
        ci_ref[...] = jnp.zeros_like(ci_ref)

    n = S5_ROWS
    mur = [mur_ref[ub:ub + 1, :] for ub in range(4)]
    mui = [mui_ref[ub:ub + 1, :] for ub in range(4)]

    def body(step, carry):
        r = (n - 1 - step) if reverse else step
        new = []
        for ub in range(4):
            cr, ci = carry[2 * ub], carry[2 * ub + 1]
            h_refs[ub][0, 0, pl.ds(r, 1), :] = cr
            h_refs[ub][0, 1, pl.ds(r, 1), :] = ci
            sr = s_refs[ub][0, 0, pl.ds(r, 1), :]
            si = s_refs[ub][0, 1, pl.ds(r, 1), :]
            new.append(mur[ub] * cr - mui[ub] * ci + sr)
            new.append(mur[ub] * ci + mui[ub] * cr + si)
        return tuple(new)

    init = tuple(ref[ub:ub + 1, :] for ub in range(4) for ref in (cr_ref, ci_ref))
    fin = lax.fori_loop(0, n, body, init, unroll=8)
    for ub in range(4):
        cr_ref[ub:ub + 1, :] = fin[2 * ub]
        ci_ref[ub:ub + 1, :] = fin[2 * ub + 1]


def _s5_scan(s_blocks, mu_r, mu_i, reverse):
    b, _, r, _ = s_blocks[0].shape
    nt = r // S5_ROWS
    d = 1 if reverse else 0
    if reverse:
        in_map = lambda bi, i: (bi, d, nt - 1 - i, 0)
        out_map = lambda bi, i: (bi, 0, nt - 1 - i, 0)
    else:
        in_map = lambda bi, i: (bi, d, i, 0)
        out_map = lambda bi, i: (bi, 0, i, 0)
    blk = (1, 2, S5_ROWS, 512)
    return pl.pallas_call(
        functools.partial(_s5_scan_kernel, reverse=reverse),
        out_shape=[jax.ShapeDtypeStruct((b, 2, r, 512), F32)] * 4,
        grid=(b, nt),
        in_specs=[pl.BlockSpec(blk, in_map)] * 4 + [pl.BlockSpec((4, 512), lambda bi, i: (0, 0))] * 2,
        out_specs=[pl.BlockSpec(blk, out_map)] * 4,
        scratch_shapes=[pltpu.VMEM((4, 512), F32), pltpu.VMEM((4, 512), F32)],
        compiler_params=pltpu.CompilerParams(dimension_semantics=("arbitrary", "arbitrary"),
                                             vmem_limit_bytes=VMEM_LIMIT),
        name="s5_scan",
    )(*s_blocks, mu_r, mu_i)


def _s5_finish_kernel(*refs):
    yt_refs, yc_refs = refs[0:4], refs[4:8]
    u_ref, z_ref, d_ref, gw_ref, gb_ref, o_ref = refs[8:]
    for t in range(S5_SUPER):
        cols = slice(t * S5_WIDTH, (t + 1) * S5_WIDTH)
        y = jnp.concatenate([yt_refs[ub][0, :, t * 128:(t + 1) * 128]
                             + yc_refs[ub][0, :, t * 128:(t + 1) * 128] for ub in range(4)], axis=1)
        y = y + d_ref[...] * u_ref[0, :, cols].astype(F32)
        y = jax.nn.gelu(y)
        gate = _dot(y.astype(BF16), gw_ref[...]) + gb_ref[...]
        y = y * _sigmoid(gate)
        o_ref[0, :, cols] = (y * _silu(z_ref[0, :, cols].astype(F32))).astype(BF16)


def s5_branch(u, z, p):
    b, l, _ = u.shape
    r = l // S5_SUPER
    wide = S5_SUPER * S5_WIDTH
    u16 = u.reshape(b, r, wide)
    z16 = z.reshape(b, r, wide)
    row_map = lambda bi, i: (bi, i, 0)
    yt, st = [], []
    for ub in range(4):
        yt.append(_s5_row_call(functools.partial(_s5_toeplitz_kernel, ub=ub), "s5_toeplitz", u16,
                               p['s5_w1'][ub], (b, r, 2048), (1, S5_ROWS, 2048), row_map))
        st.append(_s5_row_call(functools.partial(_s5_states_kernel, ub=ub), "s5_states", u16,
                               p['s5_w2'][ub], (b, 4, r, 512), (1, 4, S5_ROWS, 512),
                               lambda bi, i: (bi, 0, i, 0)))
    hf = _s5_scan(st, p['s5_mu'][0], p['s5_mu'][1], reverse=False)
    hb = _s5_scan(st, p['s5_mu'][2], p['s5_mu'][3], reverse=True)
    hblk = pl.BlockSpec((1, 2, S5_ROWS, 512), lambda bi, i: (bi, 0, i, 0))
    yc = []
    for ub in range(4):
        w3 = p['s5_w3'][ub]
        yc.append(pl.pallas_call(
            _s5_carry_kernel,
            out_shape=jax.ShapeDtypeStruct((b, r, 2048), F32),
            grid=(b, r // S5_ROWS),
            in_specs=[hblk, hblk, pl.BlockSpec(w3.shape, lambda bi, i: (0, 0))],
            out_specs=pl.BlockSpec((1, S5_ROWS, 2048), row_map),
            compiler_params=pltpu.CompilerParams(dimension_semantics=("parallel", "parallel"),
                                                 vmem_limit_bytes=VMEM_LIMIT),
            name="s5_carry",
        )(hf[ub], hb[ub], w3))
    fr = S5_ROWS // 4
    yblk = pl.BlockSpec((1, fr, 2048), row_map)
    wblk = pl.BlockSpec((1, fr, wide), row_map)
    c2 = lambda bi, i: (0, 0)
    out = pl.pallas_call(
        _s5_finish_kernel,
        out_shape=jax.ShapeDtypeStruct((b, r, wide), BF16),
        grid=(b, r // fr),
        in_specs=[yblk] * 8 + [wblk, wblk, pl.BlockSpec(p['s5_d'].shape, c2),
                               pl.BlockSpec(p['glu_w'].shape, c2), pl.BlockSpec(p['glu_b'].shape, c2)],
        out_specs=wblk,
        compiler_params=pltpu.CompilerParams(dimension_semantics=("parallel", "parallel"),
                                             vmem_limit_bytes=VMEM_LIMIT),
        name="s5_finish",
    )(*yt, *yc, u16, z16, p['s5_d'], p['glu_w'], p['glu_b'])
    return out.reshape(b, l, S5_WIDTH)


def _s5_discretise(a_re, a_im, log_dt, b_re, b_im):
    dt = jnp.exp(log_dt.astype(F32))[:, None]
    ar = a_re.astype(F32)
    ai = a_im.astype(F32)
    mag = jnp.exp(ar * dt)
    lr = mag * jnp.cos(ai * dt)
    li = mag * jnp.sin(ai * dt)
    den = ar * ar + ai * ai
    nr = lr - 1.0
    qr = (nr * ar + li * ai) / den
    qi = (li * ar - nr * ai) / den
    br = b_re.astype(F32)
    bi = b_im.astype(F32)
    bbr = qr[..., None] * br - qi[..., None] * bi
    bbi = qr[..., None] * bi + qi[..., None] * br
    return ar * dt, ai * dt, bbr, bbi


def _s5_row_weights(par):
    t = jnp.arange(S5_SUPER, dtype=F32)
    eye = jnp.eye(8, dtype=F32)
    ks, ins, outs, mus = [], [], [], []
    for d in range(2):
        a_re, a_im, log_dt, b_re, b_im, c_re, c_im = par[d]
        la, th, bbr, bbi = _s5_discretise(a_re, a_im, log_dt, b_re, b_im)
        cr = c_re.astype(F32)
        ci = c_im.astype(F32)

        def lam_pow(e):
            m = jnp.exp(e[:, None, None] * la)
            return m * jnp.cos(e[:, None, None] * th), m * jnp.sin(e[:, None, None] * th)

        pr, pi = lam_pow(t)
        xr = pr[..., None] * bbr - pi[..., None] * bbi
        xi = pr[..., None] * bbi + pi[..., None] * bbr
        ks.append(jnp.einsum('gdn,lgnc->lgdc', cr, xr) - jnp.einsum('gdn,lgnc->lgdc', ci, xi))
        er, ei = lam_pow((S5_SUPER - 1.0 - t) if d == 0 else t)
        ins.append(jnp.stack([er[..., None] * bbr - ei[..., None] * bbi,
                              er[..., None] * bbi + ei[..., None] * bbr]))
        zr, zi = lam_pow((t + 1.0) if d == 0 else (S5_SUPER - t))
        outs.append(jnp.stack([cr[None] * zr[:, :, None, :] - ci[None] * zi[:, :, None, :],
                               -(cr[None] * zi[:, :, None, :] + ci[None] * zr[:, :, None, :])]))
        mr, mi = lam_pow(jnp.full((1,), float(S5_SUPER), F32))
        mus += [mr.reshape(4, 512), mi.reshape(4, 512)]

    lag = jnp.arange(S5_SUPER)[None, :] - jnp.arange(S5_SUPER)[:, None]
    kf = jnp.where((lag >= 0)[:, :, None, None, None], ks[0][jnp.clip(lag, 0, S5_SUPER - 1)], 0.0)
    kb = jnp.where((lag <= 0)[:, :, None, None, None], ks[1][jnp.clip(-lag, 0, S5_SUPER - 1)], 0.0)
    toep = (kf + kb).transpose(2, 0, 4, 1, 3)
    toep = toep.reshape(4, 8, S5_SUPER, S5_GROUP, S5_SUPER, S5_GROUP)
    w1 = jnp.einsum('ugtcsd,gh->utgcshd', toep, eye).reshape(4, 2048, 2048)
    v2 = jnp.stack(ins)
    v2 = v2.reshape(2, 2, S5_SUPER, 4, 8, S5_STATE, S5_GROUP)
    w2 = jnp.einsum('drtugnc,gh->utgcdrhn', v2, eye).reshape(4, 2048, 2048)
    v3 = jnp.stack(outs)
    v3 = v3.reshape(2, 2, S5_SUPER, 4, 8, S5_GROUP, S5_STATE)
    w3 = jnp.einsum('drsugcn,gh->udrgnshc', v3, eye).reshape(4, 2048, 2048)
    return w1.astype(BF16), w2.astype(BF16), w3.astype(BF16), mus


def _rw_prep_kernel(x_ref, xp_ref, xn_ref, mu_ref, aup_ref, wup0_ref, wup1_ref,
                    a0_ref, w00_ref, w01_ref, kk_ref, ka_ref, ones_ref,
                    r_o, k_o, v_o, kk_o, b_o, l0_o, l1_o, *, tile):
    i = pl.program_id(1)
    nt = pl.num_programs(1)
    x = x_ref[0].astype(F32)
    rowi = lax.broadcasted_iota(jnp.int32, x.shape, 0)
    halo = RW_HALO
    prev_row = xp_ref[0][halo - 1:halo, :].astype(F32) * jnp.where(i > 0, 1.0, 0.0)
    next_row = xn_ref[0][0:1, :].astype(F32) * jnp.where(i < nt - 1, 1.0, 0.0)
    x_prev = jnp.where(rowi == 0, prev_row, pltpu.roll(x, 1, 0))
    x_next = jnp.where(rowi == tile - 1, next_row, pltpu.roll(x, tile - 1, 0))
    nb = 0.5 * (x_prev + x_next)
    xs = x + mu_ref[...] * (nb - x)
    r = xs[:, 0:512]
    k = xs[:, 512:1024]
    v = xs[:, 1024:1536]
    lt = xs[:, 1536:1664]
    a = _sigmoid(a0_ref[...] + _dot(lt.astype(BF16), aup_ref[...]))
    tw = jnp.tanh(lt).astype(BF16)
    c = math.exp(-0.5)
    l0 = -c * _sigmoid(w00_ref[...] + _dot(tw, wup0_ref[...]))
    l1 = -c * _sigmoid(w01_ref[...] + _dot(tw, wup1_ref[...]))
    kk = k * kk_ref[...]
    s1, s2 = _split2(kk * kk)
    ss = _dot(s1, ones_ref[...]) + _dot(s2, ones_ref[...])
    kk = kk / jnp.maximum(jnp.sqrt(ss), 1e-12)
    r_o[0] = r.astype(BF16)
    k_o[0] = (k * (1.0 + (a - 1.0) * ka_ref[...])).astype(BF16)
    v_o[0] = v.astype(BF16)
    kk_o[0] = kk.astype(BF16)
    b_o[0] = (kk * a).astype(BF16)
    l0_o[0] = l0
    l1_o[0] = l1


def rw_prep(h_rw, p, tile=256):
    b, l, w = h_rw.shape
    nt = l // tile
    tb = tile // RW_HALO
    tok = lambda bi, i: (bi, i, 0)
    c2 = lambda bi, i: (0, 0)
    params = [p['mu'], p['aup'], p['wup0'], p['wup1'], p['a0'], p['w00'], p['w01'],
              p['kk'], p['ka'], p['ones']]
    in_specs = [pl.BlockSpec((1, tile, w), tok),
                pl.BlockSpec((1, RW_HALO, w), lambda bi, i: (bi, jnp.maximum(i * tb - 1, 0), 0)),
                pl.BlockSpec((1, RW_HALO, w),
                             lambda bi, i: (bi, jnp.minimum((i + 1) * tb, nt * tb - 1), 0))]
    in_specs += [pl.BlockSpec(a.shape, c2) for a in params]
    return pl.pallas_call(
        functools.partial(_rw_prep_kernel, tile=tile),
        out_shape=[jax.ShapeDtypeStruct((b, l, RW_WIDTH), d) for d in (BF16,) * 5 + (F32,) * 2],
        grid=(b, nt),
        in_specs=in_specs,
        out_specs=[pl.BlockSpec((1, tile, RW_WIDTH), tok)] * 7,
        compiler_params=pltpu.CompilerParams(dimension_semantics=("parallel", "parallel"),
                                             vmem_limit_bytes=VMEM_LIMIT),
        name="rw_prep",
    )(h_rw, h_rw, h_rw, *params)


def _rw_block_diag(x, bmask):
    xb = x.astype(BF16)
    return jnp.where(bmask, jnp.concatenate([xb, xb, xb, xb], axis=0), jnp.zeros((), BF16))


def _rw_chunk_prepare(r, k, v, kk, b, lw, masks, reverse):
    tri, strict, incl, bmask, eye = masks
    c = RW_CHUNK
    bd = functools.partial(_rw_block_diag, bmask=bmask)

    l1, l2, l3 = _split3(lw)
    cl = _dot(tri, l1) + _dot(tri, l2) + _dot(tri, l3)
    yield
    cl_end = cl[0:1, :] if reverse else cl[c - 1:c, :]
    e_neg = jnp.exp(-cl)
    e_end = jnp.exp(cl_end - cl)
    kkt = kk * jnp.exp(cl - lw)
    rt = r * jnp.exp(cl)
    lhs = jnp.concatenate([kkt, rt], axis=0).astype(BF16)
    ab = _dot_nt(lhs, bd(b * e_neg))
    ak = _dot_nt(lhs, bd(k * e_neg))
    yield
    a_bb = jnp.where(strict, ab[:c], 0.0)
    a_rb = jnp.where(incl, ab[c:], 0.0).astype(BF16)
    a_bk = jnp.where(strict, ak[:c], 0.0).astype(BF16)
    a_rk = jnp.where(incl, ak[c:], 0.0).astype(BF16)
    bdv = bd(v)

    pw = -a_bb
    t = eye + pw
    pw = _dot(pw.astype(BF16), bd(pw))
    avk = _dot(jnp.concatenate([a_bk, a_rk], axis=0), bdv)
    av = avk[:c]
    yield
    for _ in range(4):
        res = _dot(jnp.concatenate([pw, t], axis=0).astype(BF16), bd(pw))
        yield
        pw = res[:c]
        t = t + res[c:]
    inv = (t + _dot(t.astype(BF16), bd(pw))).astype(BF16)
    yield
    un = -_dot(inv, jnp.concatenate([bd(av), bd(kkt)], axis=1))
    yield
    u0 = un[:, :RW_GROUP]
    ng = un[:, RW_GROUP:].astype(BF16)
    bde = (b * e_end).astype(BF16)
    ar = _dot(a_rb, jnp.concatenate([bd(u0), bd(ng)], axis=1))
    y0 = ar[:, :RW_GROUP] + avk[c:]
    rq = (rt + ar[:, RW_GROUP:]).astype(BF16)
    trans = _dot_tn(ng, bde)
    drive = _dot_tn(jnp.concatenate([u0.astype(BF16), v.astype(BF16)], axis=0),
                    jnp.concatenate([bde, (k * e_end).astype(BF16)], axis=0))
    yield
    trans = jnp.where(bmask, trans, 0.0).astype(BF16)
    drive = jnp.where(bmask, drive, 0.0)
    return rq, y0, trans, drive, jnp.exp(cl_end)


def _rw_apply(tiles, preps, s_ref, y_ref):
    for (rows, g, lanes), (rq, y0, trans, drive, p_end) in zip(tiles, preps):
        s0 = s_ref[g]
        s0b = s0.astype(BF16)
        s_ref[g] = s0 * p_end + _dot(s0b, trans) + drive
        y_ref[rows, lanes] = y0 + _dot_nt(rq, s0b)
        if g == RW_WIDTH // RW_GROUP - 1:
            yield


def _rw_masks(reverse):
    c, g = RW_CHUNK, RW_GROUP
    r2 = lax.broadcasted_iota(jnp.int32, (c, c), 0)
    c2 = lax.broadcasted_iota(jnp.int32, (c, c), 1)
    tri = jnp.where((c2 >= r2) if reverse else (c2 <= r2), 1.0, 0.0).astype(BF16)
    t = lax.broadcasted_iota(jnp.int32, (c, g), 0)
    s = lax.broadcasted_iota(jnp.int32, (c, g), 1) & (c - 1)
    strict = (s > t) if reverse else (s < t)
    incl = (s >= t) if reverse else (s <= t)
    eye = jnp.where(s == t, 1.0, 0.0).astype(F32)
    bi = lax.broadcasted_iota(jnp.int32, (g, g), 0) >> 6
    bj = lax.broadcasted_iota(jnp.int32, (g, g), 1) >> 6
    return tri, strict, incl, bi == bj, eye


def _rw_scan_kernel(*refs, reverse, final):
    if final:
        (r_ref, k_ref, v_ref, kk_ref, b_ref, lw_ref, yprev_ref, z_ref,
         rk_ref, lng_ref, lnb_ref, ones_ref, o_ref, s_ref, y_ref) = refs
    else:
        (r_ref, k_ref, v_ref, kk_ref, b_ref, lw_ref, o_ref, s_ref, y_ref) = refs

    @pl.when(pl.program_id(1) == 0)
    def _():
        s_ref[...] = jnp.zeros_like(s_ref)

    masks = _rw_masks(reverse)
    n_chunks = RW_TILE // RW_CHUNK
    order = list(range(n_chunks - 1, -1, -1) if reverse else range(n_chunks))
    n_groups = RW_WIDTH // RW_GROUP
    pending = None
    for w in range(0, n_chunks, RW_WAVE):
        tiles = [(slice(ci * RW_CHUNK, (ci + 1) * RW_CHUNK), g, slice(g * RW_GROUP, (g + 1) * RW_GROUP))
                 for ci in order[w:w + RW_WAVE] for g in range(n_groups)]
        load = lambda ref, rows, lanes: ref[0, rows, lanes].astype(F32)
        gens = [_rw_chunk_prepare(load(r_ref, rows, lanes), load(k_ref, rows, lanes),
                                  load(v_ref, rows, lanes), load(kk_ref, rows, lanes),
                                  load(b_ref, rows, lanes), lw_ref[0, rows, lanes], masks, reverse)
                for rows, g, lanes in tiles]
        if pending is not None:
            gens.append(_rw_apply(*pending, s_ref, y_ref))
        preps = _lockstep(gens)[:len(tiles)]
        pending = (tiles, preps)
    _lockstep([_rw_apply(*pending, s_ref, y_ref)])

    if not final:
        o_ref[0] = y_ref[...].astype(BF16)
        return
    ones = ones_ref[...]
    inv_n = 1.0 / RW_HEAD

    def head_sum(x):
        return _dot(x.astype(BF16), ones)

    y = y_ref[...] + yprev_ref[0].astype(F32)
    mean = head_sum(y) * inv_n
    d = y - mean
    var = head_sum(d * d) * inv_n
    yn = d * lax.rsqrt(var + RW_LN_EPS) * lng_ref[...] + lnb_ref[...]
    bonus = head_sum(r_ref[0].astype(F32) * k_ref[0].astype(F32) * rk_ref[...]) * v_ref[0].astype(F32)
    o_ref[0] = ((yn + bonus) * _silu(z_ref[0].astype(F32))).astype(BF16)


def rw_sweep(seqs, reverse, final_args=None):
    b, l, _ = seqs[0].shape
    nt = l // RW_TILE
    final = final_args is not None
    if reverse:
        tok = lambda bi, i: (bi, nt - 1 - i, 0)
    else:
        tok = lambda bi, i: (bi, i, 0)
    c2 = lambda bi, i: (0, 0)
    tok_spec = pl.BlockSpec((1, RW_TILE, RW_WIDTH), tok)
    in_specs = [tok_spec] * 6
    args = list(seqs)
    if final:
        yprev, z, rk, lng, lnb, ones = final_args
        in_specs += [tok_spec, tok_spec]
        in_specs += [pl.BlockSpec(a.shape, c2) for a in (rk, lng, lnb, ones)]
        args += [yprev, z, rk, lng, lnb, ones]
    return pl.pallas_call(
        functools.partial(_rw_scan_kernel, reverse=reverse, final=final),
        out_shape=jax.ShapeDtypeStruct((b, l, RW_WIDTH), BF16),
        grid=(b, nt),
        in_specs=in_specs,
        out_specs=tok_spec,
        scratch_shapes=[pltpu.VMEM((RW_WIDTH // RW_GROUP, RW_GROUP, RW_GROUP), F32),
                        pltpu.VMEM((RW_TILE, RW_WIDTH), F32)],
        compiler_params=pltpu.CompilerParams(dimension_semantics=("arbitrary", "arbitrary"),
                                             vmem_limit_bytes=VMEM_LIMIT),
        name="rw_final" if final else "rw_sweep",
    )(*args)


AT_HEAD_ORDER = tuple(8 * p + 4 * par + j for p in range(2) for j in range(4) for par in range(2))
LOG2E = math.log2(math.e)


def _head_rms(t, ones_ref, expand_ref, gain_ref):
    ss = _dot((t * t).astype(BF16), ones_ref[...])
    r1, r2 = _split2(lax.rsqrt(ss * (1.0 / AT_HEAD_DIM) + EPS))
    rb = _dot(r1, expand_ref[...]) + _dot(r2, expand_ref[...])
    return (t * rb * gain_ref[...]).astype(BF16)


def _proj_in_attn_kernel(x_ref, g_ref, w_ref, oq_ref, eq_ref, gq_ref, ok_ref, ek_ref, gk_ref,
                         q_o, k_o, v_o, z_o):
    x = x_ref[...]
    ms = jnp.mean(x * x, axis=-1, keepdims=True)
    h = (x * lax.rsqrt(ms + EPS) * g_ref[...]).astype(BF16)
    c0, c1, c2 = AT_WIDTH, AT_WIDTH + AT_KV_WIDTH, AT_WIDTH + 2 * AT_KV_WIDTH
    q_o[...] = _head_rms(_dot(h, w_ref[:, :c0]), oq_ref, eq_ref, gq_ref)
    k_o[...] = _head_rms(_dot(h, w_ref[:, c0:c1]), ok_ref, ek_ref, gk_ref)
    v_o[...] = _dot(h, w_ref[:, c1:c2]).astype(BF16)
    z_o[...] = _dot(h, w_ref[:, c2:]).astype(BF16)


def proj_in_attn(x2d, p):
    m = x2d.shape[0]
    n = p['w_in'].shape[1]
    consts = [p['ones_q'], p['exp_q'], p['gain_q'], p['ones_k'], p['exp_k'], p['gain_k']]
    widths = (AT_WIDTH, AT_KV_WIDTH, AT_KV_WIDTH, AT_WIDTH)
    dtypes = (BF16, BF16, BF16, BF16)
    return pl.pallas_call(
        _proj_in_attn_kernel,
        out_shape=[jax.ShapeDtypeStruct((m, w), d) for w, d in zip(widths, dtypes)],
        grid=(m // TOK_TILE,),
        in_specs=[pl.BlockSpec((TOK_TILE, D_MODEL), lambda i: (i, 0)),
                  pl.BlockSpec((1, D_MODEL), lambda i: (0, 0)),
                  pl.BlockSpec((D_MODEL, n), lambda i: (0, 0))]
                 + [pl.BlockSpec(c.shape, lambda i: (0, 0)) for c in consts],
        out_specs=[pl.BlockSpec((TOK_TILE, w), lambda i: (i, 0)) for w in widths],
        compiler_params=pltpu.CompilerParams(dimension_semantics=("parallel",),
                                             vmem_limit_bytes=VMEM_LIMIT),
        name="proj_in_attn",
    )(x2d, p['norm'].reshape(1, D_MODEL), p['w_in'], *consts)


def _attn_kernel(sink_ref, q_ref, kp_ref, kc_ref, kn_ref, vp_ref, vc_ref, vn_ref, z_ref, bias_ref,
                 o_ref):
    blk = AT_BLOCK
    lane = lax.broadcasted_iota(jnp.int32, (3 * blk, 2 * AT_HEAD_DIM), 1)
    even_lane = lane < AT_HEAD_DIM
    zero = jnp.zeros((), BF16)

    def tile_chain(tile, k2, v2):
        cols = slice(tile * 128, (tile + 1) * 128)
        h_even, h_odd = AT_HEAD_ORDER[2 * tile], AT_HEAD_ORDER[2 * tile + 1]
        s = _dot_nt(q_ref[0][:, cols], k2)
        yield
        s = s + bias_ref[0, tile]
        se, so = s[:, :3 * blk], s[:, 3 * blk:]
        sink_e = sink_ref[h_even] * LOG2E
        sink_o = sink_ref[h_odd] * LOG2E
        me = jnp.maximum(jnp.max(se, axis=-1, keepdims=True), sink_e)
        mo = jnp.maximum(jnp.max(so, axis=-1, keepdims=True), sink_o)
        yield
        pe = jnp.exp2(se - me)
        po = jnp.exp2(so - mo)
        de = jnp.sum(pe, axis=-1, keepdims=True) + jnp.exp2(sink_e - me)
        do = jnp.sum(po, axis=-1, keepdims=True) + jnp.exp2(sink_o - mo)
        yield
        pcat = jnp.concatenate([(pe * (1.0 / de)).astype(BF16), (po * (1.0 / do)).astype(BF16)], axis=1)
        o = _dot(pcat, v2)
        yield
        o_ref[0, :, cols] = (o * _silu(z_ref[0, :, cols].astype(F32))).astype(o_ref.dtype)

    chains = []
    for p in range(2):
        lanes = slice(p * 128, (p + 1) * 128)
        kt = jnp.concatenate([kp_ref[0][:, lanes], kc_ref[0][:, lanes], kn_ref[0][:, lanes]], axis=0)
        vt = jnp.concatenate([vp_ref[0][:, lanes], vc_ref[0][:, lanes], vn_ref[0][:, lanes]], axis=0)
        k2 = jnp.concatenate([jnp.where(even_lane, kt, zero), jnp.where(even_lane, zero, kt)], axis=0)
        v2 = jnp.concatenate([jnp.where(even_lane, vt, zero), jnp.where(even_lane, zero, vt)], axis=0)
        chains += [tile_chain(4 * p + j, k2, v2) for j in range(4)]
    _lockstep(chains)


def _attn_bias():
    blk = AT_BLOCK
    iq = jnp.arange(blk)[:, None]
    jk = jnp.arange(3 * blk)[None, :]
    absrel = jnp.abs(blk + iq - jk)
    in_win = absrel <= blk
    slopes = LOG2E * jnp.exp2(-8.0 * (jnp.asarray(AT_HEAD_ORDER, F32) + 1.0) / AT_HEADS)
    alibi = -slopes[:, None, None] * absrel.astype(F32)[None]
    variants = []
    for key_lo, key_hi in ((blk, 3 * blk), (0, 3 * blk), (0, 2 * blk)):
        ok = in_win & (jk >= key_lo) & (jk < key_hi)
        b = jnp.where(ok[None], alibi, NEG_INF)
        variants.append(jnp.concatenate([b[0::2], b[1::2]], axis=-1))
    return jnp.stack(variants)


def attention(q, k, v, z, sink):
    b, l, _ = q.shape
    nb = l // AT_BLOCK
    assert nb >= 2
    cur = lambda bi, i, s: (bi, i, 0)
    prv = lambda bi, i, s: (bi, jnp.maximum(i - 1, 0), 0)
    nxt = lambda bi, i, s: (bi, jnp.minimum(i + 1, nb - 1), 0)
    edge = lambda bi, i, s: (jnp.where(i == 0, 0, jnp.where(i == nb - 1, 2, 1)), 0, 0, 0)
    kv = lambda f: pl.BlockSpec((1, AT_BLOCK, AT_KV_WIDTH), f)
    wide = pl.BlockSpec((1, AT_BLOCK, AT_WIDTH), cur)
    grid_spec = pltpu.PrefetchScalarGridSpec(
        num_scalar_prefetch=1,
        grid=(b, nb),
        in_specs=[wide, kv(prv), kv(cur), kv(nxt), kv(prv), kv(cur), kv(nxt), wide,
                  pl.BlockSpec((1, 8, AT_BLOCK, 6 * AT_BLOCK), edge)],
        out_specs=wide,
    )
    return pl.pallas_call(
        _attn_kernel,
        out_shape=jax.ShapeDtypeStruct((b, l, AT_WIDTH), BF16),
        grid_spec=grid_spec,
        compiler_params=pltpu.CompilerParams(dimension_semantics=("parallel", "parallel"),
                                             vmem_limit_bytes=VMEM_LIMIT),
        name="attn",
    )(sink, q, k, k, k, v, v, v, z, _attn_bias())


def _odd_params(j, od_norm, od_w_in, at_q_norm, at_k_norm, at_sink, od_w_out):
    cols = jnp.concatenate([jnp.arange(AT_HEAD_DIM) + h * AT_HEAD_DIM for h in AT_HEAD_ORDER])
    w = od_w_in[j]
    c0, c2 = AT_WIDTH, AT_WIDTH + 2 * AT_KV_WIDTH
    w_in = jnp.concatenate([w[:, :c0][:, cols], w[:, c0:c2], w[:, c2:][:, cols]], axis=1)

    def head_maps(width):
        head = jnp.arange(width) // AT_HEAD_DIM
        ones = (head[:, None] == jnp.arange(128)[None, :]).astype(BF16)
        return ones, ones.T

    ones_q, exp_q = head_maps(AT_WIDTH)
    ones_k, exp_k = head_maps(AT_KV_WIDTH)
    qscale = (AT_HEAD_DIM ** -0.5) * LOG2E
    return {'norm': od_norm[j], 'w_in': w_in.astype(BF16),
            'ones_q': ones_q, 'exp_q': exp_q, 'ones_k': ones_k, 'exp_k': exp_k,
            'gain_q': (jnp.tile(at_q_norm[j].astype(F32), AT_HEADS) * qscale).reshape(1, -1),
            'gain_k': jnp.tile(at_k_norm[j].astype(F32), AT_KV_HEADS).reshape(1, -1),
            'sink': at_sink[j].astype(F32), 'w_out': od_w_out[j][cols, :].astype(BF16)}


def _block_ones(width, block):
    i = jnp.arange(width) // block
    return (i[:, None] == i[None, :]).astype(BF16)


def _even_params(j, ev):
    g = lambda name: ev[name][j]
    row = lambda a: a.astype(F32).reshape(1, -1)
    p = {}
    p['norm'] = g('norm')
    p['w_in'] = g('w_in').astype(BF16)
    p['w_out_a'] = g('w_out')[:S5_WIDTH].astype(BF16)
    p['w_out_b'] = g('w_out')[S5_WIDTH:].astype(BF16)
    par = [tuple(g(n)[d] for n in ('s5_a_re', 's5_a_im', 's5_log_dt', 's5_b_re', 's5_b_im',
                                   's5_c_re', 's5_c_im')) for d in range(2)]
    p['s5_w1'], p['s5_w2'], p['s5_w3'], p['s5_mu'] = _s5_row_weights(par)
    p['s5_d'] = row(g('s5_d'))
    p['glu_w'] = g('s5_glu_w').astype(BF16)
    p['glu_b'] = row(g('s5_glu_b'))
    zeros64 = jnp.zeros((64, RW_WIDTH), F32)
    p['mu'] = row(g('rw_mu'))
    p['aup'] = jnp.concatenate([zeros64, g('rw_a_up').astype(F32)], axis=0).astype(BF16)
    p['wup0'] = jnp.concatenate([g('rw_w_up')[0].astype(F32), zeros64], axis=0).astype(BF16)
    p['wup1'] = jnp.concatenate([g('rw_w_up')[1].astype(F32), zeros64], axis=0).astype(BF16)
    p['a0'] = row(g('rw_a0'))
    p['w00'] = row(g('rw_w0')[0])
    p['w01'] = row(g('rw_w0')[1])
    p['kk'] = row(g('rw_k_k'))
    p['ka'] = row(g('rw_k_a'))
    p['ones'] = _block_ones(RW_WIDTH, RW_HEAD)
    p['rk'] = row(g('rw_r_k'))
    p['lng'] = row(g('rw_ln_g'))
    p['lnb'] = row(g('rw_ln_b'))
    return p


def _even_layer(x, p):
    b, l, _ = x.shape
    x2 = x.reshape(b * l, D_MODEL)
    u, z_s5, h_rw, z_rw = proj_in(x2, p['norm'], p['w_in'],
                                  (S5_WIDTH, S5_WIDTH, RW_SHIFTED, RW_WIDTH))
    u = u.reshape(b, l, -1)
    z_s5 = z_s5.reshape(b, l, -1)
    h_rw = h_rw.reshape(b, l, -1)
    z_rw = z_rw.reshape(b, l, -1)

    ya = s5_branch(u, z_s5, p)

    r, k, v, kk, bb, l0, l1 = rw_prep(h_rw, p)
    yb_rev = rw_sweep((r, k, v, kk, bb, l1), reverse=True)
    yb = rw_sweep((r, k, v, kk, bb, l0), reverse=False,
                  final_args=(yb_rev, z_rw, p['rk'], p['lng'], p['lnb'], p['ones']))

    out = proj_out(x2, [ya.reshape(b * l, -1), yb.reshape(b * l, -1)],
                   [p['w_out_a'], p['w_out_b']])
    return out.reshape(b, l, D_MODEL)


def _odd_layer(x, p):
    b, l, _ = x.shape
    x2 = x.reshape(b * l, D_MODEL)
    q, k, v, z = proj_in_attn(x2, p)
    sh = lambda t: t.reshape(b, l, -1)
    o = attention(sh(q), sh(k), sh(v), sh(z), p['sink'])
    out = proj_out(x2, [o.reshape(b * l, -1)], [p['w_out']])
    return out.reshape(b, l, D_MODEL)


def kernel(x_prompt, x_sample, ev_norm, ev_w_in, s5_a_re, s5_a_im, s5_log_dt, s5_b_re, s5_b_im, s5_c_re, s5_c_im, s5_d, s5_glu_w, s5_glu_b, rw_mu, rw_w0, rw_w_up, rw_a0, rw_a_up, rw_k_k, rw_k_a, rw_r_k, rw_ln_g, rw_ln_b, ev_w_out, od_norm, od_w_in, at_q_norm, at_k_norm, at_sink, od_w_out):
    ev = {'norm': ev_norm, 'w_in': ev_w_in, 's5_a_re': s5_a_re, 's5_a_im': s5_a_im,
          's5_log_dt': s5_log_dt, 's5_b_re': s5_b_re, 's5_b_im': s5_b_im,
          's5_c_re': s5_c_re, 's5_c_im': s5_c_im, 's5_d': s5_d, 's5_glu_w': s5_glu_w,
          's5_glu_b': s5_glu_b, 'rw_mu': rw_mu, 'rw_w0': rw_w0, 'rw_w_up': rw_w_up,
          'rw_a0': rw_a0, 'rw_a_up': rw_a_up, 'rw_k_k': rw_k_k, 'rw_k_a': rw_k_a,
          'rw_r_k': rw_r_k, 'rw_ln_g': rw_ln_g, 'rw_ln_b': rw_ln_b, 'w_out': ev_w_out}
    n_even = ev_norm.shape[0]
    n_odd = od_norm.shape[0]
    even = [_even_params(j, ev) for j in range(n_even)]
    odd = [_odd_params(j, od_norm, od_w_in, at_q_norm, at_k_norm, at_sink, od_w_out)
           for j in range(n_odd)]

    def trunk(x):
        for layer in range(n_even + n_odd):
            j = layer // 2
            x = _even_layer(x, even[j]) if layer % 2 == 0 else _odd_layer(x, odd[j])
        return x

    return (trunk(x_prompt), trunk(x_sample))
```

```python
import functools
import math

import jax
import jax.numpy as jnp
from jax import lax
from jax.experimental import pallas as pl
from jax.experimental.pallas import tpu as pltpu

F32 = jnp.float32
BF16 = jnp.bfloat16

D_MODEL = 1024
EPS = 1e-6

S5_WIDTH = 512
S5_GROUP = 16
S5_GROUPS = 32
S5_STATE = 64
S5_SUPER = 16
S5_ROWS = 256

RW_WIDTH = 512
RW_HEAD = 64
RW_SHIFTED = 3 * RW_WIDTH + 128
RW_LN_EPS = 64e-5
RW_CHUNK = 64
RW_TILE = 512
RW_WAVE = 4
RW_GROUP = 256
RW_HALO = 16

AT_HEADS = 16
AT_KV_HEADS = 4
AT_GROUP = 4
AT_HEAD_DIM = 64
AT_BLOCK = 128
AT_WIDTH = 1024
AT_KV_WIDTH = 256
NEG_INF = -1e30

TOK_TILE = 512
VMEM_LIMIT = 48 * 1024 * 1024


def _dot(a, b):
    return jnp.dot(a, b, preferred_element_type=F32)


def _dot_nt(a, b):
    return lax.dot_general(a, b, (((1,), (1,)), ((), ())), preferred_element_type=F32)


def _dot_tn(a, b):
    return lax.dot_general(a, b, (((0,), (0,)), ((), ())), preferred_element_type=F32)


def _lockstep(gens):
    gens = list(gens)
    results = [None] * len(gens)
    live = list(range(len(gens)))
    while live:
        still = []
        for i in live:
            try:
                next(gens[i])
                still.append(i)
            except StopIteration as done:
                results[i] = done.value
        live = still
    return results


def _split3(x):
    x1 = x.astype(BF16)
    r1 = x - x1.astype(F32)
    x2 = r1.astype(BF16)
    x3 = (r1 - x2.astype(F32)).astype(BF16)
    return x1, x2, x3


def _split2(x):
    x1 = x.astype(BF16)
    x2 = (x - x1.astype(F32)).astype(BF16)
    return x1, x2


def _sigmoid(x):
    return 1.0 / (1.0 + jnp.exp(-x))


def _silu(x):
    return x * _sigmoid(x)


def _proj_in_kernel(x_ref, g_ref, w_ref, *out_refs, splits):
    x = x_ref[...]
    ms = jnp.mean(x * x, axis=-1, keepdims=True)
    h = (x * lax.rsqrt(ms + EPS) * g_ref[...]).astype(BF16)
    off = 0
    for o_ref, n in zip(out_refs, splits):
        o_ref[...] = _dot(h, w_ref[:, off:off + n]).astype(o_ref.dtype)
        off += n


def proj_in(x2d, g, w_bf16, splits):
    m = x2d.shape[0]
    n = w_bf16.shape[1]
    assert sum(splits) == n and m % TOK_TILE == 0
    return pl.pallas_call(
        functools.partial(_proj_in_kernel, splits=splits),
        out_shape=[jax.ShapeDtypeStruct((m, s), BF16) for s in splits],
        grid=(m // TOK_TILE,),
        in_specs=[pl.BlockSpec((TOK_TILE, D_MODEL), lambda i: (i, 0)),
                  pl.BlockSpec((1, D_MODEL), lambda i: (0, 0)),
                  pl.BlockSpec((D_MODEL, n), lambda i: (0, 0))],
        out_specs=[pl.BlockSpec((TOK_TILE, s), lambda i: (i, 0)) for s in splits],
        compiler_params=pltpu.CompilerParams(dimension_semantics=("parallel",),
                                             vmem_limit_bytes=VMEM_LIMIT),
        name="proj_in",
    )(x2d, g.reshape(1, D_MODEL), w_bf16)


def _proj_out_kernel(*refs, n_in):
    x_ref = refs[0]
    y_refs = refs[1:1 + n_in]
    w_refs = refs[1 + n_in:1 + 2 * n_in]
    o_ref = refs[1 + 2 * n_in]
    acc = x_ref[...]
    for y_ref, w_ref in zip(y_refs, w_refs):
        acc = acc + _dot(y_ref[...].astype(BF16), w_ref[...])
    o_ref[...] = acc


def proj_out(x2d, ys, ws_bf16):
    m = x2d.shape[0]
    n_in = len(ys)
    in_specs = [pl.BlockSpec((TOK_TILE, D_MODEL), lambda i: (i, 0))]
    in_specs += [pl.BlockSpec((TOK_TILE, y.shape[1]), lambda i: (i, 0)) for y in ys]
    in_specs += [pl.BlockSpec(w.shape, lambda i: (0, 0)) for w in ws_bf16]
    return pl.pallas_call(
        functools.partial(_proj_out_kernel, n_in=n_in),
        out_shape=jax.ShapeDtypeStruct((m, D_MODEL), F32),
        grid=(m // TOK_TILE,),
        in_specs=in_specs,
        out_specs=pl.BlockSpec((TOK_TILE, D_MODEL), lambda i: (i, 0)),
        compiler_params=pltpu.CompilerParams(dimension_semantics=("parallel",),
                                             vmem_limit_bytes=VMEM_LIMIT),
        name="proj_out",
    )(x2d, *ys, *ws_bf16)


def _s5_row_tiles(u_ref, ub):
    return jnp.concatenate(
        [u_ref[0, :, t * S5_WIDTH + ub * 128:t * S5_WIDTH + (ub + 1) * 128] for t in range(S5_SUPER)],
        axis=1)


def _s5_toeplitz_kernel(u_ref, w_ref, o_ref, *, ub):
    o_ref[0] = _dot(_s5_row_tiles(u_ref, ub), w_ref[...])


def _s5_states_kernel(u_ref, w_ref, o_ref, *, ub):
    res = _dot(_s5_row_tiles(u_ref, ub), w_ref[...])
    for q in range(4):
        o_ref[0, q] = res[:, q * 512:(q + 1) * 512]


def _s5_carry_kernel(hf_ref, hb_ref, w_ref, o_ref):
    h = jnp.concatenate([hf_ref[0, 0], hf_ref[0, 1], hb_ref[0, 0], hb_ref[0, 1]], axis=1)
    o_ref[0] = _dot(h.astype(BF16), w_ref[...])


def _s5_row_call(kernel, name, u16, w, out_shape, out_block, out_map):
    b, r, _ = u16.shape
    return pl.pallas_call(
        kernel,
        out_shape=jax.ShapeDtypeStruct(out_shape, F32),
        grid=(b, r // S5_ROWS),
        in_specs=[pl.BlockSpec((1, S5_ROWS, S5_SUPER * S5_WIDTH), lambda bi, i: (bi, i, 0)),
                  pl.BlockSpec(w.shape, lambda bi, i: (0, 0))],
        out_specs=pl.BlockSpec(out_block, out_map),
        compiler_params=pltpu.CompilerParams(dimension_semantics=("parallel", "parallel"),
                                             vmem_limit_bytes=VMEM_LIMIT),
        name=name,
    )(u16, w)


def _s5_scan_kernel(*refs, reverse):
    s_refs, (mur_ref, mui_ref), h_refs, (cr_ref, ci_ref) = refs[0:4], refs[4:6], refs[6:10], refs[10:12]

    @pl.when(pl.program_id(1) == 0)
    def _():
        cr_ref[...] = jnp.zeros_like(cr_ref)
        ci_ref[...] = jnp.zeros_like(ci_ref)

    n = S5_ROWS
    mur = [mur_ref[ub:ub + 1, :] for ub in range(4)]
    mui = [mui_ref[ub:ub + 1, :] for ub in range(4)]

    def body(step, carry):
        r = (n - 1 - step) if reverse else step
        new = []
        for ub in range(4):
            cr, ci = carry[2 * ub], carry[2 * ub + 1]
            h_refs[ub][0, 0, pl.ds(r, 1), :] = cr
            h_refs[ub][0, 1, pl.ds(r, 1), :] = ci
            sr = s_refs[ub][0, 0, pl.ds(r, 1), :]
            si = s_refs[ub][0, 1, pl.ds(r, 1), :]
            new.append(mur[ub] * cr - mui[ub] * ci + sr)
            new.append(mur[ub] * ci + mui[ub] * cr + si)
        return tuple(new)

    init = tuple(ref[ub:ub + 1, :] for ub in range(4) for ref in (cr_ref, ci_ref))
    fin = lax.fori_loop(0, n, body, init, unroll=8)
    for ub in range(4):
        cr_ref[ub:ub + 1, :] = fin[2 * ub]
        ci_ref[ub:ub + 1, :] = fin[2 * ub + 1]


def _s5_scan(s_blocks, mu_r, mu_i, reverse):
    b, _, r, _ = s_blocks[0].shape
    nt = r // S5_ROWS
    d = 1 if reverse else 0
    if reverse:
        in_map = lambda bi, i: (bi, d, nt - 1 - i, 0)
        out_map = lambda bi, i: (bi, 0, nt - 1 - i, 0)
    else:
        in_map = lambda bi, i: (bi, d, i, 0)
        out_map = lambda bi, i: (bi, 0, i, 0)
    blk = (1, 2, S5_ROWS, 512)
    return pl.pallas_call(
        functools.partial(_s5_scan_kernel, reverse=reverse),
        out_shape=[jax.ShapeDtypeStruct((b, 2, r, 512), F32)] * 4,
        grid=(b, nt),
        in_specs=[pl.BlockSpec(blk, in_map)] * 4 + [pl.BlockSpec((4, 512), lambda bi, i: (0, 0))] * 2,
        out_specs=[pl.BlockSpec(blk, out_map)] * 4,
        scratch_shapes=[pltpu.VMEM((4, 512), F32), pltpu.VMEM((4, 512), F32)],
        compiler_params=pltpu.CompilerParams(dimension_semantics=("arbitrary", "arbitrary"),
                                             vmem_limit_bytes=VMEM_LIMIT),
        name="s5_scan",
    )(*s_blocks, mu_r, mu_i)


def _s5_finish_kernel(*refs):
    yt_refs, yc_refs = refs[0:4], refs[4:8]
    u_ref, z_ref, d_ref, gw_ref, gb_ref, o_ref = refs[8:]
    for t in range(S5_SUPER):
        cols = slice(t * S5_WIDTH, (t + 1) * S5_WIDTH)
        y = jnp.concatenate([yt_refs[ub][0, :, t * 128:(t + 1) * 128]
                             + yc_refs[ub][0, :, t * 128:(t + 1) * 128] for ub in range(4)], axis=1)
        y = y + d_ref[...] * u_ref[0, :, cols].astype(F32)
        y = jax.nn.gelu(y)
        gate = _dot(y.astype(BF16), gw_ref[...]) + gb_ref[...]
        y = y * _sigmoid(gate)
        o_ref[0, :, cols] = (y * _silu(z_ref[0, :, cols].astype(F32))).astype(BF16)


def s5_branch(u, z, p):
    b, l, _ = u.shape
    r = l // S5_SUPER
    wide = S5_SUPER * S5_WIDTH
    u16 = u.reshape(b, r, wide)
    z16 = z.reshape(b, r, wide)
    row_map = lambda bi, i: (bi, i, 0)
    yt, st = [], []
    for ub in range(4):
        yt.append(_s5_row_call(functools.partial(_s5_toeplitz_kernel, ub=ub), "s5_toeplitz", u16,
                               p['s5_w1'][ub], (b, r, 2048), (1, S5_ROWS, 2048), row_map))
        st.append(_s5_row_call(functools.partial(_s5_states_kernel, ub=ub), "s5_states", u16,
                               p['s5_w2'][ub], (b, 4, r, 512), (1, 4, S5_ROWS, 512),
                               lambda bi, i: (bi, 0, i, 0)))
    hf = _s5_scan(st, p['s5_mu'][0], p['s5_mu'][1], reverse=False)
    hb = _s5_scan(st, p['s5_mu'][2], p['s5_mu'][3], reverse=True)
    hblk = pl.BlockSpec((1, 2, S5_ROWS, 512), lambda bi, i: (bi, 0, i, 0))
    yc = []
    for ub in range(4):
        w3 = p['s5_w3'][ub]
        yc.append(pl.pallas_call(
            _s5_carry_kernel,
            out_shape=jax.ShapeDtypeStruct((b, r, 2048), F32),
            grid=(b, r // S5_ROWS),
            in_specs=[hblk, hblk, pl.BlockSpec(w3.shape, lambda bi, i: (0, 0))],
            out_specs=pl.BlockSpec((1, S5_ROWS, 2048), row_map),
            compiler_params=pltpu.CompilerParams(dimension_semantics=("parallel", "parallel"),
                                                 vmem_limit_bytes=VMEM_LIMIT),
            name="s5_carry",
        )(hf[ub], hb[ub], w3))
    fr = S5_ROWS // 4
    yblk = pl.BlockSpec((1, fr, 2048), row_map)
    wblk = pl.BlockSpec((1, fr, wide), row_map)
    c2 = lambda bi, i: (0, 0)
    out = pl.pallas_call(
        _s5_finish_kernel,
        out_shape=jax.ShapeDtypeStruct((b, r, wide), BF16),
        grid=(b, r // fr),
        in_specs=[yblk] * 8 + [wblk, wblk, pl.BlockSpec(p['s5_d'].shape, c2),
                               pl.BlockSpec(p['glu_w'].shape, c2), pl.BlockSpec(p['glu_b'].shape, c2)],
        out_specs=wblk,
        compiler_params=pltpu.CompilerParams(dimension_semantics=("parallel", "parallel"),
                                             vmem_limit_bytes=VMEM_LIMIT),
        name="s5_finish",
    )(*yt, *yc, u16, z16, p['s5_d'], p['glu_w'], p['glu_b'])
    return out.reshape(b, l, S5_WIDTH)


def _s5_discretise(a_re, a_im, log_dt, b_re, b_im):
    dt = jnp.exp(log_dt.astype(F32))[:, None]
    ar = a_re.astype(F32)
    ai = a_im.astype(F32)
    mag = jnp.exp(ar * dt)
    lr = mag * jnp.cos(ai * dt)
    li = mag * jnp.sin(ai * dt)
    den = ar * ar + ai * ai
    nr = lr - 1.0
    qr = (nr * ar + li * ai) / den
    qi = (li * ar - nr * ai) / den
    br = b_re.astype(F32)
    bi = b_im.astype(F32)
    bbr = qr[..., None] * br - qi[..., None] * bi
    bbi = qr[..., None] * bi + qi[..., None] * br
    return ar * dt, ai * dt, bbr, bbi


def _s5_row_weights(par):
    t = jnp.arange(S5_SUPER, dtype=F32)
    ks, ins, outs, mus = [], [], [], []
    for d in range(2):
        a_re, a_im, log_dt, b_re, b_im, c_re, c_im = par[d]
        la, th, bbr, bbi = _s5_discretise(a_re, a_im, log_dt, b_re, b_im)
        cr = c_re.astype(F32)
        ci = c_im.astype(F32)

        def lam_pow(e):
            m = jnp.exp(e[:, None, None] * la)
            return m * jnp.cos(e[:, None, None] * th), m * jnp.sin(e[:, None, None] * th)

        pr, pi = lam_pow(t)
        xr = pr[..., None] * bbr - pi[..., None] * bbi
        xi = pr[..., None] * bbi + pi[..., None] * bbr
        ks.append(jnp.einsum('gdn,lgnc->lgdc', cr, xr) - jnp.einsum('gdn,lgnc->lgdc', ci, xi))
        er, ei = lam_pow((S5_SUPER - 1.0 - t) if d == 0 else t)
        ins.append(jnp.stack([er[..., None] * bbr - ei[..., None] * bbi,
                              er[..., None] * bbi + ei[..., None] * bbr]))
        zr, zi = lam_pow((t + 1.0) if d == 0 else (S5_SUPER - t))
        outs.append(jnp.stack([cr[None] * zr[:, :, None, :] - ci[None] * zi[:, :, None, :],
                               -(cr[None] * zi[:, :, None, :] + ci[None] * zr[:, :, None, :])]))
        mr, mi = lam_pow(jnp.full((1,), float(S5_SUPER), F32))
        mus += [mr.reshape(4, 512), mi.reshape(4, 512)]

    def same_group(rows, per_row, cols, per_col):
        r = jnp.arange(rows)[:, None] // per_row
        c = jnp.arange(cols)[None, :] // per_col
        return r == c

    lags = jnp.concatenate([ks[1][::-1], ks[0][1:]], axis=0)
    lags = lags.at[S5_SUPER - 1].add(ks[0][0])
    blk = lags.transpose(1, 3, 0, 2).reshape(4, 128, 2 * S5_SUPER - 1, S5_GROUP)
    blk = jnp.tile(blk.astype(BF16), (1, 1, 1, 8))
    blk = jnp.where(same_group(128, S5_GROUP, 128, S5_GROUP)[None, :, None, :], blk, 0)
    lag_ix = (jnp.arange(S5_SUPER)[None, :] - jnp.arange(S5_SUPER)[:, None] + S5_SUPER - 1).reshape(-1)
    w1 = jnp.take(blk, lag_ix, axis=2)
    w1 = w1.reshape(4, 128, S5_SUPER, S5_SUPER, 128).transpose(0, 2, 1, 3, 4).reshape(4, 2048, 2048)

    v2 = jnp.stack(ins)
    v2 = v2.reshape(2, 2, S5_SUPER, 4, 8, S5_STATE, S5_GROUP).transpose(3, 2, 4, 6, 0, 1, 5)
    v2 = v2.reshape(4, S5_SUPER, 128, 2, 2, S5_STATE).astype(BF16)
    v2 = jnp.tile(v2, (1, 1, 1, 1, 1, 8))
    v2 = jnp.where(same_group(128, S5_GROUP, 512, S5_STATE)[None, None, :, None, None, :], v2, 0)
    w2 = v2.reshape(4, 2048, 2048)

    v3 = jnp.stack(outs)
    v3 = v3.reshape(2, 2, S5_SUPER, 4, 8, S5_GROUP, S5_STATE).transpose(3, 0, 1, 4, 6, 2, 5)
    v3 = v3.reshape(4, 2, 2, 512, S5_SUPER, S5_GROUP).astype(BF16)
    v3 = jnp.tile(v3, (1, 1, 1, 1, 1, 8))
    v3 = jnp.where(same_group(512, S5_STATE, 128, S5_GROUP)[None, None, None, :, None, :], v3, 0)
    w3 = v3.reshape(4, 2048, 2048)
    return w1, w2, w3, mus


def _rw_prep_kernel(x_ref, xp_ref, xn_ref, mu_ref, aup_ref, wup0_ref, wup1_ref,
                    a0_ref, w00_ref, w01_ref, kk_ref, ka_ref, ones_ref,
                    r_o, k_o, v_o, kk_o, b_o, l0_o, l1_o, *, tile):
    i = pl.program_id(1)
    nt = pl.num_programs(1)
    x = x_ref[0].astype(F32)
    rowi = lax.broadcasted_iota(jnp.int32, x.shape, 0)
    halo = RW_HALO
    prev_row = xp_ref[0][halo - 1:halo, :].astype(F32) * jnp.where(i > 0, 1.0, 0.0)
    next_row = xn_ref[0][0:1, :].astype(F32) * jnp.where(i < nt - 1, 1.0, 0.0)
    x_prev = jnp.where(rowi == 0, prev_row, pltpu.roll(x, 1, 0))
    x_next = jnp.where(rowi == tile - 1, next_row, pltpu.roll(x, tile - 1, 0))
    nb = 0.5 * (x_prev + x_next)
    xs = x + mu_ref[...] * (nb - x)
    r = xs[:, 0:512]
    k = xs[:, 512:1024]
    v = xs[:, 1024:1536]
    lt = xs[:, 1536:1664]
    a = _sigmoid(a0_ref[...] + _dot(lt.astype(BF16), aup_ref[...]))
    tw = jnp.tanh(lt).astype(BF16)
    c = math.exp(-0.5)
    l0 = -c * _sigmoid(w00_ref[...] + _dot(tw, wup0_ref[...]))
    l1 = -c * _sigmoid(w01_ref[...] + _dot(tw, wup1_ref[...]))
    kk = k * kk_ref[...]
    s1, s2 = _split2(kk * kk)
    ss = _dot(s1, ones_ref[...]) + _dot(s2, ones_ref[...])
    kk = kk / jnp.maximum(jnp.sqrt(ss), 1e-12)
    r_o[0] = r.astype(BF16)
    k_o[0] = (k * (1.0 + (a - 1.0) * ka_ref[...])).astype(BF16)
    v_o[0] = v.astype(BF16)
    kk_o[0] = kk.astype(BF16)
    b_o[0] = (kk * a).astype(BF16)
    l0_o[0] = l0
    l1_o[0] = l1


def rw_prep(h_rw, p, tile=256):
    b, l, w = h_rw.shape
    nt = l // tile
    tb = tile // RW_HALO
    tok = lambda bi, i: (bi, i, 0)
    c2 = lambda bi, i: (0, 0)
    params = [p['mu'], p['aup'], p['wup0'], p['wup1'], p['a0'], p['w00'], p['w01'],
              p['kk'], p['ka'], p['ones']]
    in_specs = [pl.BlockSpec((1, tile, w), tok),
                pl.BlockSpec((1, RW_HALO, w), lambda bi, i: (bi, jnp.maximum(i * tb - 1, 0), 0)),
                pl.BlockSpec((1, RW_HALO, w),
                             lambda bi, i: (bi, jnp.minimum((i + 1) * tb, nt * tb - 1), 0))]
    in_specs += [pl.BlockSpec(a.shape, c2) for a in params]
    return pl.pallas_call(
        functools.partial(_rw_prep_kernel, tile=tile),
        out_shape=[jax.ShapeDtypeStruct((b, l, RW_WIDTH), d) for d in (BF16,) * 5 + (F32,) * 2],
        grid=(b, nt),
        in_specs=in_specs,
        out_specs=[pl.BlockSpec((1, tile, RW_WIDTH), tok)] * 7,
        compiler_params=pltpu.CompilerParams(dimension_semantics=("parallel", "parallel"),
                                             vmem_limit_bytes=VMEM_LIMIT),
        name="rw_prep",
    )(h_rw, h_rw, h_rw, *params)


def _rw_block_diag(x, bmask):
    xb = x.astype(BF16)
    return jnp.where(bmask, jnp.concatenate([xb, xb, xb, xb], axis=0), jnp.zeros((), BF16))


def _rw_chunk_prepare(r, k, v, kk, b, lw, masks, reverse):
    tri, strict, incl, bmask, eye = masks
    c = RW_CHUNK
    bd = functools.partial(_rw_block_diag, bmask=bmask)

    l1, l2, l3 = _split3(lw)
    cl = _dot(tri, l1) + _dot(tri, l2) + _dot(tri, l3)
    yield
    cl_end = cl[0:1, :] if reverse else cl[c - 1:c, :]
    e_neg = jnp.exp(-cl)
    e_end = jnp.exp(cl_end - cl)
    kkt = kk * jnp.exp(cl - lw)
    rt = r * jnp.exp(cl)
    lhs = jnp.concatenate([kkt, rt], axis=0).astype(BF16)
    ab = _dot_nt(lhs, bd(b * e_neg))
    ak = _dot_nt(lhs, bd(k * e_neg))
    yield
    a_bb = jnp.where(strict, ab[:c], 0.0)
    a_rb = jnp.where(incl, ab[c:], 0.0).astype(BF16)
    a_bk = jnp.where(strict, ak[:c], 0.0).astype(BF16)
    a_rk = jnp.where(incl, ak[c:], 0.0).astype(BF16)
    bdv = bd(v)

    pw = -a_bb
    t = eye + pw
    pw = _dot(pw.astype(BF16), bd(pw))
    avk = _dot(jnp.concatenate([a_bk, a_rk], axis=0), bdv)
    av = avk[:c]
    yield
    for _ in range(4):
        res = _dot(jnp.concatenate([pw, t], axis=0).astype(BF16), bd(pw))
        yield
        pw = res[:c]
        t = t + res[c:]
    inv = (t + _dot(t.astype(BF16), bd(pw))).astype(BF16)
    yield
    un = -_dot(inv, jnp.concatenate([bd(av), bd(kkt)], axis=1))
    yield
    u0 = un[:, :RW_GROUP]
    ng = un[:, RW_GROUP:].astype(BF16)
    bde = (b * e_end).astype(BF16)
    ar = _dot(a_rb, jnp.concatenate([bd(u0), bd(ng)], axis=1))
    y0 = ar[:, :RW_GROUP] + avk[c:]
    rq = (rt + ar[:, RW_GROUP:]).astype(BF16)
    trans = _dot_tn(ng, bde)
    drive = _dot_tn(jnp.concatenate([u0.astype(BF16), v.astype(BF16)], axis=0),
                    jnp.concatenate([bde, (k * e_end).astype(BF16)], axis=0))
    yield
    trans = jnp.where(bmask, trans, 0.0).astype(BF16)
    drive = jnp.where(bmask, drive, 0.0)
    return rq, y0, trans, drive, jnp.exp(cl_end)


def _rw_apply(tiles, preps, s_ref, y_ref):
    for (rows, g, lanes), (rq, y0, trans, drive, p_end) in zip(tiles, preps):
        s0 = s_ref[g]
        s0b = s0.astype(BF16)
        s_ref[g] = s0 * p_end + _dot(s0b, trans) + drive
        y_ref[rows, lanes] = y0 + _dot_nt(rq, s0b)
        if g == RW_WIDTH // RW_GROUP - 1:
            yield


def _rw_masks(reverse):
    c, g = RW_CHUNK, RW_GROUP
    r2 = lax.broadcasted_iota(jnp.int32, (c, c), 0)
    c2 = lax.broadcasted_iota(jnp.int32, (c, c), 1)
    tri = jnp.where((c2 >= r2) if reverse else (c2 <= r2), 1.0, 0.0).astype(BF16)
    t = lax.broadcasted_iota(jnp.int32, (c, g), 0)
    s = lax.broadcasted_iota(jnp.int32, (c, g), 1) & (c - 1)
    strict = (s > t) if reverse else (s < t)
    incl = (s >= t) if reverse else (s <= t)
    eye = jnp.where(s == t, 1.0, 0.0).astype(F32)
    bi = lax.broadcasted_iota(jnp.int32, (g, g), 0) >> 6
    bj = lax.broadcasted_iota(jnp.int32, (g, g), 1) >> 6
    return tri, strict, incl, bi == bj, eye


def _rw_scan_kernel(*refs, reverse, final):
    if final:
        (r_ref, k_ref, v_ref, kk_ref, b_ref, lw_ref, yprev_ref, z_ref,
         rk_ref, lng_ref, lnb_ref, ones_ref, o_ref, s_ref, y_ref) = refs
    else:
        (r_ref, k_ref, v_ref, kk_ref, b_ref, lw_ref, o_ref, s_ref, y_ref) = refs

    @pl.when(pl.program_id(1) == 0)
    def _():
        s_ref[...] = jnp.zeros_like(s_ref)

    masks = _rw_masks(reverse)
    n_chunks = RW_TILE // RW_CHUNK
    order = list(range(n_chunks - 1, -1, -1) if reverse else range(n_chunks))
    n_groups = RW_WIDTH // RW_GROUP
    pending = None
    for w in range(0, n_chunks, RW_WAVE):
        tiles = [(slice(ci * RW_CHUNK, (ci + 1) * RW_CHUNK), g, slice(g * RW_GROUP, (g + 1) * RW_GROUP))
                 for ci in order[w:w + RW_WAVE] for g in range(n_groups)]
        load = lambda ref, rows, lanes: ref[0, rows, lanes].astype(F32)
        gens = [_rw_chunk_prepare(load(r_ref, rows, lanes), load(k_ref, rows, lanes),
                                  load(v_ref, rows, lanes), load(kk_ref, rows, lanes),
                                  load(b_ref, rows, lanes), lw_ref[0, rows, lanes], masks, reverse)
                for rows, g, lanes in tiles]
        if pending is not None:
            gens.append(_rw_apply(*pending, s_ref, y_ref))
        preps = _lockstep(gens)[:len(tiles)]
        pending = (tiles, preps)
    _lockstep([_rw_apply(*pending, s_ref, y_ref)])

    if not final:
        o_ref[0] = y_ref[...].astype(BF16)
        return
    ones = ones_ref[...]
    inv_n = 1.0 / RW_HEAD

    def head_sum(x):
        return _dot(x.astype(BF16), ones)

    y = y_ref[...] + yprev_ref[0].astype(F32)
    mean = head_sum(y) * inv_n
    d = y - mean
    var = head_sum(d * d) * inv_n
    yn = d * lax.rsqrt(var + RW_LN_EPS) * lng_ref[...] + lnb_ref[...]
    bonus = head_sum(r_ref[0].astype(F32) * k_ref[0].astype(F32) * rk_ref[...]) * v_ref[0].astype(F32)
    o_ref[0] = ((yn + bonus) * _silu(z_ref[0].astype(F32))).astype(BF16)


def rw_sweep(seqs, reverse, final_args=None):
    b, l, _ = seqs[0].shape
    nt = l // RW_TILE
    final = final_args is not None
    if reverse:
        tok = lambda bi, i: (bi, nt - 1 - i, 0)
    else:
        tok = lambda bi, i: (bi, i, 0)
    c2 = lambda bi, i: (0, 0)
    tok_spec = pl.BlockSpec((1, RW_TILE, RW_WIDTH), tok)
    in_specs = [tok_spec] * 6
    args = list(seqs)
    if final:
        yprev, z, rk, lng, lnb, ones = final_args
        in_specs += [tok_spec, tok_spec]
        in_specs += [pl.BlockSpec(a.shape, c2) for a in (rk, lng, lnb, ones)]
        args += [yprev, z, rk, lng, lnb, ones]
    return pl.pallas_call(
        functools.partial(_rw_scan_kernel, reverse=reverse, final=final),
        out_shape=jax.ShapeDtypeStruct((b, l, RW_WIDTH), BF16),
        grid=(b, nt),
        in_specs=in_specs,
        out_specs=tok_spec,
        scratch_shapes=[pltpu.VMEM((RW_WIDTH // RW_GROUP, RW_GROUP, RW_GROUP), F32),
                        pltpu.VMEM((RW_TILE, RW_WIDTH), F32)],
        compiler_params=pltpu.CompilerParams(dimension_semantics=("arbitrary", "arbitrary"),
                                             vmem_limit_bytes=VMEM_LIMIT),
        name="rw_final" if final else "rw_sweep",
    )(*args)


AT_HEAD_ORDER = tuple(8 * p + 4 * par + j for p in range(2) for j in range(4) for par in range(2))
LOG2E = math.log2(math.e)


def _head_rms(t, ones_ref, expand_ref, gain_ref):
    ss = _dot((t * t).astype(BF16), ones_ref[...])
    r1, r2 = _split2(lax.rsqrt(ss * (1.0 / AT_HEAD_DIM) + EPS))
    rb = _dot(r1, expand_ref[...]) + _dot(r2, expand_ref[...])
    return (t * rb * gain_ref[...]).astype(BF16)


def _proj_in_attn_kernel(x_ref, g_ref, w_ref, oq_ref, eq_ref, gq_ref, ok_ref, ek_ref, gk_ref,
                         q_o, k_o, v_o, z_o):
    x = x_ref[...]
    ms = jnp.mean(x * x, axis=-1, keepdims=True)
    h = (x * lax.rsqrt(ms + EPS) * g_ref[...]).astype(BF16)
    c0, c1, c2 = AT_WIDTH, AT_WIDTH + AT_KV_WIDTH, AT_WIDTH + 2 * AT_KV_WIDTH
    q_o[...] = _head_rms(_dot(h, w_ref[:, :c0]), oq_ref, eq_ref, gq_ref)
    k_o[...] = _head_rms(_dot(h, w_ref[:, c0:c1]), ok_ref, ek_ref, gk_ref)
    v_o[...] = _dot(h, w_ref[:, c1:c2]).astype(BF16)
    z_o[...] = _dot(h, w_ref[:, c2:]).astype(BF16)


def proj_in_attn(x2d, p):
    m = x2d.shape[0]
    n = p['w_in'].shape[1]
    consts = [p['ones_q'], p['exp_q'], p['gain_q'], p['ones_k'], p['exp_k'], p['gain_k']]
    widths = (AT_WIDTH, AT_KV_WIDTH, AT_KV_WIDTH, AT_WIDTH)
    dtypes = (BF16, BF16, BF16, BF16)
    return pl.pallas_call(
        _proj_in_attn_kernel,
        out_shape=[jax.ShapeDtypeStruct((m, w), d) for w, d in zip(widths, dtypes)],
        grid=(m // TOK_TILE,),
        in_specs=[pl.BlockSpec((TOK_TILE, D_MODEL), lambda i: (i, 0)),
                  pl.BlockSpec((1, D_MODEL), lambda i: (0, 0)),
                  pl.BlockSpec((D_MODEL, n), lambda i: (0, 0))]
                 + [pl.BlockSpec(c.shape, lambda i: (0, 0)) for c in consts],
        out_specs=[pl.BlockSpec((TOK_TILE, w), lambda i: (i, 0)) for w in widths],
        compiler_params=pltpu.CompilerParams(dimension_semantics=("parallel",),
                                             vmem_limit_bytes=VMEM_LIMIT),
        name="proj_in_attn",
    )(x2d, p['norm'].reshape(1, D_MODEL), p['w_in'], *consts)


def _attn_kernel(sink_ref, q_ref, kp_ref, kc_ref, kn_ref, vp_ref, vc_ref, vn_ref, z_ref, bias_ref,
                 o_ref):
    blk = AT_BLOCK
    lane = lax.broadcasted_iota(jnp.int32, (3 * blk, 2 * AT_HEAD_DIM), 1)
    even_lane = lane < AT_HEAD_DIM
    zero = jnp.zeros((), BF16)

    def tile_chain(tile, k2, v2):
        cols = slice(tile * 128, (tile + 1) * 128)
        h_even, h_odd = AT_HEAD_ORDER[2 * tile], AT_HEAD_ORDER[2 * tile + 1]
        s = _dot_nt(q_ref[0][:, cols], k2)
        yield
        s = s + bias_ref[0, tile]
        se, so = s[:, :3 * blk], s[:, 3 * blk:]
        sink_e = sink_ref[h_even] * LOG2E
        sink_o = sink_ref[h_odd] * LOG2E
        me = jnp.maximum(jnp.max(se, axis=-1, keepdims=True), sink_e)
        mo = jnp.maximum(jnp.max(so, axis=-1, keepdims=True), sink_o)
        yield
        pe = jnp.exp2(se - me)
        po = jnp.exp2(so - mo)
        de = jnp.sum(pe, axis=-1, keepdims=True) + jnp.exp2(sink_e - me)
        do = jnp.sum(po, axis=-1, keepdims=True) + jnp.exp2(sink_o - mo)
        yield
        pcat = jnp.concatenate([(pe * (1.0 / de)).astype(BF16), (po * (1.0 / do)).astype(BF16)], axis=1)
        o = _dot(pcat, v2)
        yield
        o_ref[0, :, cols] = (o * _silu(z_ref[0, :, cols].astype(F32))).astype(o_ref.dtype)

    chains = []
    for p in range(2):
        lanes = slice(p * 128, (p + 1) * 128)
        kt = jnp.concatenate([kp_ref[0][:, lanes], kc_ref[0][:, lanes], kn_ref[0][:, lanes]], axis=0)
        vt = jnp.concatenate([vp_ref[0][:, lanes], vc_ref[0][:, lanes], vn_ref[0][:, lanes]], axis=0)
        k2 = jnp.concatenate([jnp.where(even_lane, kt, zero), jnp.where(even_lane, zero, kt)], axis=0)
        v2 = jnp.concatenate([jnp.where(even_lane, vt, zero), jnp.where(even_lane, zero, vt)], axis=0)
        chains += [tile_chain(4 * p + j, k2, v2) for j in range(4)]
    _lockstep(chains)


def _attn_bias():
    blk = AT_BLOCK
    iq = jnp.arange(blk)[:, None]
    jk = jnp.arange(3 * blk)[None, :]
    absrel = jnp.abs(blk + iq - jk)
    in_win = absrel <= blk
    slopes = LOG2E * jnp.exp2(-8.0 * (jnp.asarray(AT_HEAD_ORDER, F32) + 1.0) / AT_HEADS)
    alibi = -slopes[:, None, None] * absrel.astype(F32)[None]
    variants = []
    for key_lo, key_hi in ((blk, 3 * blk), (0, 3 * blk), (0, 2 * blk)):
        ok = in_win & (jk >= key_lo) & (jk < key_hi)
        b = jnp.where(ok[None], alibi, NEG_INF)
        variants.append(jnp.concatenate([b[0::2], b[1::2]], axis=-1))
    return jnp.stack(variants)


def attention(q, k, v, z, sink):
    b, l, _ = q.shape
    nb = l // AT_BLOCK
    assert nb >= 2
    cur = lambda bi, i, s: (bi, i, 0)
    prv = lambda bi, i, s: (bi, jnp.maximum(i - 1, 0), 0)
    nxt = lambda bi, i, s: (bi, jnp.minimum(i + 1, nb - 1), 0)
    edge = lambda bi, i, s: (jnp.where(i == 0, 0, jnp.where(i == nb - 1, 2, 1)), 0, 0, 0)
    kv = lambda f: pl.BlockSpec((1, AT_BLOCK, AT_KV_WIDTH), f)
    wide = pl.BlockSpec((1, AT_BLOCK, AT_WIDTH), cur)
    grid_spec = pltpu.PrefetchScalarGridSpec(
        num_scalar_prefetch=1,
        grid=(b, nb),
        in_specs=[wide, kv(prv), kv(cur), kv(nxt), kv(prv), kv(cur), kv(nxt), wide,
                  pl.BlockSpec((1, 8, AT_BLOCK, 6 * AT_BLOCK), edge)],
        out_specs=wide,
    )
    return pl.pallas_call(
        _attn_kernel,
        out_shape=jax.ShapeDtypeStruct((b, l, AT_WIDTH), BF16),
        grid_spec=grid_spec,
        compiler_params=pltpu.CompilerParams(dimension_semantics=("parallel", "parallel"),
                                             vmem_limit_bytes=VMEM_LIMIT),
        name="attn",
    )(sink, q, k, k, k, v, v, v, z, _attn_bias())


def _odd_params(j, od_norm, od_w_in, at_q_norm, at_k_norm, at_sink, od_w_out):
    cols = jnp.concatenate([jnp.arange(AT_HEAD_DIM) + h * AT_HEAD_DIM for h in AT_HEAD_ORDER])
    w = od_w_in[j]
    c0, c2 = AT_WIDTH, AT_WIDTH + 2 * AT_KV_WIDTH
    w_in = jnp.concatenate([w[:, :c0][:, cols], w[:, c0:c2], w[:, c2:][:, cols]], axis=1)

    def head_maps(width):
        head = jnp.arange(width) // AT_HEAD_DIM
        ones = (head[:, None] == jnp.arange(128)[None, :]).astype(BF16)
        return ones, ones.T

    ones_q, exp_q = head_maps(AT_WIDTH)
    ones_k, exp_k = head_maps(AT_KV_WIDTH)
    qscale = (AT_HEAD_DIM ** -0.5) * LOG2E
    return {'norm': od_norm[j], 'w_in': w_in.astype(BF16),
            'ones_q': ones_q, 'exp_q': exp_q, 'ones_k': ones_k, 'exp_k': exp_k,
            'gain_q': (jnp.tile(at_q_norm[j].astype(F32), AT_HEADS) * qscale).reshape(1, -1),
            'gain_k': jnp.tile(at_k_norm[j].astype(F32), AT_KV_HEADS).reshape(1, -1),
            'sink': at_sink[j].astype(F32), 'w_out': od_w_out[j][cols, :].astype(BF16)}


def _block_ones(width, block):
    i = jnp.arange(width) // block
    return (i[:, None] == i[None, :]).astype(BF16)


def _even_params(j, ev):
    g = lambda name: ev[name][j]
    row = lambda a: a.astype(F32).reshape(1, -1)
    p = {}
    p['norm'] = g('norm')
    p['w_in'] = g('w_in').astype(BF16)
    p['w_out_a'] = g('w_out')[:S5_WIDTH].astype(BF16)
    p['w_out_b'] = g('w_out')[S5_WIDTH:].astype(BF16)
    par = [tuple(g(n)[d] for n in ('s5_a_re', 's5_a_im', 's5_log_dt', 's5_b_re', 's5_b_im',
                                   's5_c_re', 's5_c_im')) for d in range(2)]
    p['s5_w1'], p['s5_w2'], p['s5_w3'], p['s5_mu'] = _s5_row_weights(par)
    p['s5_d'] = row(g('s5_d'))
    p['glu_w'] = g('s5_glu_w').astype(BF16)
    p['glu_b'] = row(g('s5_glu_b'))
    zeros64 = jnp.zeros((64, RW_WIDTH), F32)
    p['mu'] = row(g('rw_mu'))
    p['aup'] = jnp.concatenate([zeros64, g('rw_a_up').astype(F32)], axis=0).astype(BF16)
    p['wup0'] = jnp.concatenate([g('rw_w_up')[0].astype(F32), zeros64], axis=0).astype(BF16)
    p['wup1'] = jnp.concatenate([g('rw_w_up')[1].astype(F32), zeros64], axis=0).astype(BF16)
    p['a0'] = row(g('rw_a0'))
    p['w00'] = row(g('rw_w0')[0])
    p['w01'] = row(g('rw_w0')[1])
    p['kk'] = row(g('rw_k_k'))
    p['ka'] = row(g('rw_k_a'))
    p['ones'] = _block_ones(RW_WIDTH, RW_HEAD)
    p['rk'] = row(g('rw_r_k'))
    p['lng'] = row(g('rw_ln_g'))
    p['lnb'] = row(g('rw_ln_b'))
    return p


def _even_layer(x, p):
    b, l, _ = x.shape
    x2 = x.reshape(b * l, D_MODEL)
    u, z_s5, h_rw, z_rw = proj_in(x2, p['norm'], p['w_in'],
                                  (S5_WIDTH, S5_WIDTH, RW_SHIFTED, RW_WIDTH))
    u = u.reshape(b, l, -1)
    z_s5 = z_s5.reshape(b, l, -1)
    h_rw = h_rw.reshape(b, l, -1)
    z_rw = z_rw.reshape(b, l, -1)

    ya = s5_branch(u, z_s5, p)

    r, k, v, kk, bb, l0, l1 = rw_prep(h_rw, p)
    yb_rev = rw_sweep((r, k, v, kk, bb, l1), reverse=True)
    yb = rw_sweep((r, k, v, kk, bb, l0), reverse=False,
                  final_args=(yb_rev, z_rw, p['rk'], p['lng'], p['lnb'], p['ones']))

    out = proj_out(x2, [ya.reshape(b * l, -1), yb.reshape(b * l, -1)],
                   [p['w_out_a'], p['w_out_b']])
    return out.reshape(b, l, D_MODEL)


def _odd_layer(x, p):
    b, l, _ = x.shape
    x2 = x.reshape(b * l, D_MODEL)
    q, k, v, z = proj_in_attn(x2, p)
    sh = lambda t: t.reshape(b, l, -1)
    o = attention(sh(q), sh(k), sh(v), sh(z), p['sink'])
    out = proj_out(x2, [o.reshape(b * l, -1)], [p['w_out']])
    return out.reshape(b, l, D_MODEL)


def kernel(x_prompt, x_sample, ev_norm, ev_w_in, s5_a_re, s5_a_im, s5_log_dt, s5_b_re, s5_b_im, s5_c_re, s5_c_im, s5_d, s5_glu_w, s5_glu_b, rw_mu, rw_w0, rw_w_up, rw_a0, rw_a_up, rw_k_k, rw_k_a, rw_r_k, rw_ln_g, rw_ln_b, ev_w_out, od_norm, od_w_in, at_q_norm, at_k_norm, at_sink, od_w_out):
    ev = {'norm': ev_norm, 'w_in': ev_w_in, 's5_a_re': s5_a_re, 's5_a_im': s5_a_im,
          's5_log_dt': s5_log_dt, 's5_b_re': s5_b_re, 's5_b_im': s5_b_im,
          's5_c_re': s5_c_re, 's5_c_im': s5_c_im, 's5_d': s5_d, 's5_glu_w': s5_glu_w,
          's5_glu_b': s5_glu_b, 'rw_mu': rw_mu, 'rw_w0': rw_w0, 'rw_w_up': rw_w_up,
          'rw_a0': rw_a0, 'rw_a_up': rw_a_up, 'rw_k_k': rw_k_k, 'rw_k_a': rw_k_a,
          'rw_r_k': rw_r_k, 'rw_ln_g': rw_ln_g, 'rw_ln_b': rw_ln_b, 'w_out': ev_w_out}
    n_even = ev_norm.shape[0]
    n_odd = od_norm.shape[0]
    even = [_even_params(j, ev) for j in range(n_even)]
    odd = [_odd_params(j, od_norm, od_w_in, at_q_norm, at_k_norm, at_sink, od_w_out)
           for j in range(n_odd)]

    def trunk(x):
        for layer in range(n_even + n_odd):
            j = layer // 2
            x = _even_layer(x, even[j]) if layer % 2 == 0 else _odd_layer(x, odd[j])
        return x

    return (trunk(x_prompt), trunk(x_sample))
```

```python
import functools
import math

import jax
import jax.numpy as jnp
from jax import lax
from jax.experimental import pallas as pl
from jax.experimental.pallas import tpu as pltpu

F32 = jnp.float32
BF16 = jnp.bfloat16

D_MODEL = 1024
EPS = 1e-6

S5_WIDTH = 512
S5_GROUP = 16
S5_GROUPS = 32
S5_STATE = 64
S5_SUPER = 16
S5_ROWS = 256

RW_WIDTH = 512
RW_HEAD = 64
RW_SHIFTED = 3 * RW_WIDTH + 128
RW_LN_EPS = 64e-5
RW_CHUNK = 64
RW_TILE = 512
RW_WAVE = 4
RW_GROUP = 256
RW_HALO = 16

AT_HEADS = 16
AT_KV_HEADS = 4
AT_GROUP = 4
AT_HEAD_DIM = 64
AT_BLOCK = 128
AT_WIDTH = 1024
AT_KV_WIDTH = 256
NEG_INF = -1e30

TOK_TILE = 512
VMEM_LIMIT = 48 * 1024 * 1024


def _dot(a, b):
    return jnp.dot(a, b, preferred_element_type=F32)


def _dot_nt(a, b):
    return lax.dot_general(a, b, (((1,), (1,)), ((), ())), preferred_element_type=F32)


def _dot_tn(a, b):
    return lax.dot_general(a, b, (((0,), (0,)), ((), ())), preferred_element_type=F32)


def _lockstep(gens):
    gens = list(gens)
    results = [None] * len(gens)
    live = list(range(len(gens)))
    while live:
        still = []
        for i in live:
            try:
                next(gens[i])
                still.append(i)
            except StopIteration as done:
                results[i] = done.value
        live = still
    return results


def _split3(x):
    x1 = x.astype(BF16)
    r1 = x - x1.astype(F32)
    x2 = r1.astype(BF16)
    x3 = (r1 - x2.astype(F32)).astype(BF16)
    return x1, x2, x3


def _split2(x):
    x1 = x.astype(BF16)
    x2 = (x - x1.astype(F32)).astype(BF16)
    return x1, x2


def _sigmoid(x):
    return 1.0 / (1.0 + jnp.exp(-x))


def _silu(x):
    return x * _sigmoid(x)


def _proj_in_kernel(x_ref, g_ref, w_ref, *out_refs, splits):
    x = x_ref[...]
    ms = jnp.mean(x * x, axis=-1, keepdims=True)
    h = (x * lax.rsqrt(ms + EPS) * g_ref[...]).astype(BF16)
    off = 0
    for o_ref, n in zip(out_refs, splits):
        o_ref[...] = _dot(h, w_ref[:, off:off + n]).astype(o_ref.dtype)
        off += n


def proj_in(x2d, g, w_bf16, splits):
    m = x2d.shape[0]
    n = w_bf16.shape[1]
    assert sum(splits) == n and m % TOK_TILE == 0
    return pl.pallas_call(
        functools.partial(_proj_in_kernel, splits=splits),
        out_shape=[jax.ShapeDtypeStruct((m, s), BF16) for s in splits],
        grid=(m // TOK_TILE,),
        in_specs=[pl.BlockSpec((TOK_TILE, D_MODEL), lambda i: (i, 0)),
                  pl.BlockSpec((1, D_MODEL), lambda i: (0, 0)),
                  pl.BlockSpec((D_MODEL, n), lambda i: (0, 0))],
        out_specs=[pl.BlockSpec((TOK_TILE, s), lambda i: (i, 0)) for s in splits],
        compiler_params=pltpu.CompilerParams(dimension_semantics=("parallel",),
                                             vmem_limit_bytes=VMEM_LIMIT),
        name="proj_in",
    )(x2d, g.reshape(1, D_MODEL), w_bf16)


def _proj_out_kernel(*refs, n_in):
    x_ref = refs[0]
    y_refs = refs[1:1 + n_in]
    w_refs = refs[1 + n_in:1 + 2 * n_in]
    o_ref = refs[1 + 2 * n_in]
    acc = x_ref[...]
    for y_ref, w_ref in zip(y_refs, w_refs):
        acc = acc + _dot(y_ref[...].astype(BF16), w_ref[...])
    o_ref[...] = acc


def proj_out(x2d, ys, ws_bf16):
    m = x2d.shape[0]
    n_in = len(ys)
    in_specs = [pl.BlockSpec((TOK_TILE, D_MODEL), lambda i: (i, 0))]
    in_specs += [pl.BlockSpec((TOK_TILE, y.shape[1]), lambda i: (i, 0)) for y in ys]
    in_specs += [pl.BlockSpec(w.shape, lambda i: (0, 0)) for w in ws_bf16]
    return pl.pallas_call(
        functools.partial(_proj_out_kernel, n_in=n_in),
        out_shape=jax.ShapeDtypeStruct((m, D_MODEL), F32),
        grid=(m // TOK_TILE,),
        in_specs=in_specs,
        out_specs=pl.BlockSpec((TOK_TILE, D_MODEL), lambda i: (i, 0)),
        compiler_params=pltpu.CompilerParams(dimension_semantics=("parallel",),
                                             vmem_limit_bytes=VMEM_LIMIT),
        name="proj_out",
    )(x2d, *ys, *ws_bf16)


def _s5_row_tiles(u_ref, ub):
    return jnp.concatenate(
        [u_ref[0, :, t * S5_WIDTH + ub * 128:t * S5_WIDTH + (ub + 1) * 128] for t in range(S5_SUPER)],
        axis=1)


def _s5_toeplitz_kernel(u_ref, blk_ref, o_ref, w_ref, *, ub):
    @pl.when((pl.program_id(0) == 0) & (pl.program_id(1) == 0))
    def _():
        for t in range(S5_SUPER):
            for tp in range(S5_SUPER):
                w_ref[t * 128:(t + 1) * 128, tp * 128:(tp + 1) * 128] = blk_ref[tp - t + S5_SUPER - 1]

    o_ref[0] = _dot(_s5_row_tiles(u_ref, ub), w_ref[...])


def _s5_states_kernel(u_ref, w_ref, o_ref, *, ub):
    res = _dot(_s5_row_tiles(u_ref, ub), w_ref[...])
    for q in range(4):
        o_ref[0, q] = res[:, q * 512:(q + 1) * 512]


def _s5_carry_kernel(hf_ref, hb_ref, w_ref, o_ref):
    h = jnp.concatenate([hf_ref[0, 0], hf_ref[0, 1], hb_ref[0, 0], hb_ref[0, 1]], axis=1)
    o_ref[0] = _dot(h.astype(BF16), w_ref[...])


def _s5_row_call(kernel, name, u16, w, out_shape, out_block, out_map, scratch=()):
    b, r, _ = u16.shape
    zeros = (0,) * w.ndim
    return pl.pallas_call(
        kernel,
        out_shape=jax.ShapeDtypeStruct(out_shape, F32),
        grid=(b, r // S5_ROWS),
        in_specs=[pl.BlockSpec((1, S5_ROWS, S5_SUPER * S5_WIDTH), lambda bi, i: (bi, i, 0)),
                  pl.BlockSpec(w.shape, lambda bi, i: zeros)],
        out_specs=pl.BlockSpec(out_block, out_map),
        scratch_shapes=list(scratch),
        compiler_params=pltpu.CompilerParams(dimension_semantics=("arbitrary", "arbitrary"),
                                             vmem_limit_bytes=VMEM_LIMIT),
        name=name,
    )(u16, w)


def _s5_scan_kernel(*refs, reverse):
    s_refs, (mur_ref, mui_ref), h_refs, (cr_ref, ci_ref) = refs[0:4], refs[4:6], refs[6:10], refs[10:12]

    @pl.when(pl.program_id(1) == 0)
    def _():
        cr_ref[...] = jnp.zeros_like(cr_ref)
        ci_ref[...] = jnp.zeros_like(ci_ref)

    n = S5_ROWS
    mur = [mur_ref[ub:ub + 1, :] for ub in range(4)]
    mui = [mui_ref[ub:ub + 1, :] for ub in range(4)]

    def body(step, carry):
        r = (n - 1 - step) if reverse else step
        new = []
        for ub in range(4):
            cr, ci = carry[2 * ub], carry[2 * ub + 1]
            h_refs[ub][0, 0, pl.ds(r, 1), :] = cr
            h_refs[ub][0, 1, pl.ds(r, 1), :] = ci
            sr = s_refs[ub][0, 0, pl.ds(r, 1), :]
            si = s_refs[ub][0, 1, pl.ds(r, 1), :]
            new.append(mur[ub] * cr - mui[ub] * ci + sr)
            new.append(mur[ub] * ci + mui[ub] * cr + si)
        return tuple(new)

    init = tuple(ref[ub:ub + 1, :] for ub in range(4) for ref in (cr_ref, ci_ref))
    fin = lax.fori_loop(0, n, body, init, unroll=8)
    for ub in range(4):
        cr_ref[ub:ub + 1, :] = fin[2 * ub]
        ci_ref[ub:ub + 1, :] = fin[2 * ub + 1]


def _s5_scan(s_blocks, mu_r, mu_i, reverse):
    b, _, r, _ = s_blocks[0].shape
    nt = r // S5_ROWS
    d = 1 if reverse else 0
    if reverse:
        in_map = lambda bi, i: (bi, d, nt - 1 - i, 0)
        out_map = lambda bi, i: (bi, 0, nt - 1 - i, 0)
    else:
        in_map = lambda bi, i: (bi, d, i, 0)
        out_map = lambda bi, i: (bi, 0, i, 0)
    blk = (1, 2, S5_ROWS, 512)
    return pl.pallas_call(
        functools.partial(_s5_scan_kernel, reverse=reverse),
        out_shape=[jax.ShapeDtypeStruct((b, 2, r, 512), F32)] * 4,
        grid=(b, nt),
        in_specs=[pl.BlockSpec(blk, in_map)] * 4 + [pl.BlockSpec((4, 512), lambda bi, i: (0, 0))] * 2,
        out_specs=[pl.BlockSpec(blk, out_map)] * 4,
        scratch_shapes=[pltpu.VMEM((4, 512), F32), pltpu.VMEM((4, 512), F32)],
        compiler_params=pltpu.CompilerParams(dimension_semantics=("arbitrary", "arbitrary"),
                                             vmem_limit_bytes=VMEM_LIMIT),
        name="s5_scan",
    )(*s_blocks, mu_r, mu_i)


def _s5_finish_kernel(*refs):
    yt_refs, yc_refs = refs[0:4], refs[4:8]
    u_ref, z_ref, d_ref, gw_ref, gb_ref, o_ref = refs[8:]
    for t in range(S5_SUPER):
        cols = slice(t * S5_WIDTH, (t + 1) * S5_WIDTH)
        y = jnp.concatenate([yt_refs[ub][0, :, t * 128:(t + 1) * 128]
                             + yc_refs[ub][0, :, t * 128:(t + 1) * 128] for ub in range(4)], axis=1)
        y = y + d_ref[...] * u_ref[0, :, cols].astype(F32)
        y = jax.nn.gelu(y)
        gate = _dot(y.astype(BF16), gw_ref[...]) + gb_ref[...]
        y = y * _sigmoid(gate)
        o_ref[0, :, cols] = (y * _silu(z_ref[0, :, cols].astype(F32))).astype(BF16)


def s5_branch(u, z, p):
    b, l, _ = u.shape
    r = l // S5_SUPER
    wide = S5_SUPER * S5_WIDTH
    u16 = u.reshape(b, r, wide)
    z16 = z.reshape(b, r, wide)
    row_map = lambda bi, i: (bi, i, 0)
    yt, st = [], []
    for ub in range(4):
        yt.append(_s5_row_call(functools.partial(_s5_toeplitz_kernel, ub=ub), "s5_toeplitz", u16,
                               p['s5_w1'][ub], (b, r, 2048), (1, S5_ROWS, 2048), row_map,
                               scratch=[pltpu.VMEM((2048, 2048), BF16)]))
        st.append(_s5_row_call(functools.partial(_s5_states_kernel, ub=ub), "s5_states", u16,
                               p['s5_w2'][ub], (b, 4, r, 512), (1, 4, S5_ROWS, 512),
                               lambda bi, i: (bi, 0, i, 0)))
    hf = _s5_scan(st, p['s5_mu'][0], p['s5_mu'][1], reverse=False)
    hb = _s5_scan(st, p['s5_mu'][2], p['s5_mu'][3], reverse=True)
    hblk = pl.BlockSpec((1, 2, S5_ROWS, 512), lambda bi, i: (bi, 0, i, 0))
    yc = []
    for ub in range(4):
        w3 = p['s5_w3'][ub]
        yc.append(pl.pallas_call(
            _s5_carry_kernel,
            out_shape=jax.ShapeDtypeStruct((b, r, 2048), F32),
            grid=(b, r // S5_ROWS),
            in_specs=[hblk, hblk, pl.BlockSpec(w3.shape, lambda bi, i: (0, 0))],
            out_specs=pl.BlockSpec((1, S5_ROWS, 2048), row_map),
            compiler_params=pltpu.CompilerParams(dimension_semantics=("parallel", "parallel"),
                                                 vmem_limit_bytes=VMEM_LIMIT),
            name="s5_carry",
        )(hf[ub], hb[ub], w3))
    fr = S5_ROWS // 2
    yblk = pl.BlockSpec((1, fr, 2048), row_map)
    wblk = pl.BlockSpec((1, fr, wide), row_map)
    c2 = lambda bi, i: (0, 0)
    out = pl.pallas_call(
        _s5_finish_kernel,
        out_shape=jax.ShapeDtypeStruct((b, r, wide), BF16),
        grid=(b, r // fr),
        in_specs=[yblk] * 8 + [wblk, wblk, pl.BlockSpec(p['s5_d'].shape, c2),
                               pl.BlockSpec(p['glu_w'].shape, c2), pl.BlockSpec(p['glu_b'].shape, c2)],
        out_specs=wblk,
        compiler_params=pltpu.CompilerParams(dimension_semantics=("parallel", "parallel"),
                                             vmem_limit_bytes=VMEM_LIMIT),
        name="s5_finish",
    )(*yt, *yc, u16, z16, p['s5_d'], p['glu_w'], p['glu_b'])
    return out.reshape(b, l, S5_WIDTH)


def _s5_discretise(a_re, a_im, log_dt, b_re, b_im):
    dt = jnp.exp(log_dt.astype(F32))[:, None]
    ar = a_re.astype(F32)
    ai = a_im.astype(F32)
    mag = jnp.exp(ar * dt)
    lr = mag * jnp.cos(ai * dt)
    li = mag * jnp.sin(ai * dt)
    den = ar * ar + ai * ai
    nr = lr - 1.0
    qr = (nr * ar + li * ai) / den
    qi = (li * ar - nr * ai) / den
    br = b_re.astype(F32)
    bi = b_im.astype(F32)
    bbr = qr[..., None] * br - qi[..., None] * bi
    bbi = qr[..., None] * bi + qi[..., None] * br
    return ar * dt, ai * dt, bbr, bbi


def _s5_row_weights(par):
    t = jnp.arange(S5_SUPER, dtype=F32)
    ks, ins, outs, mus = [], [], [], []
    for d in range(2):
        a_re, a_im, log_dt, b_re, b_im, c_re, c_im = par[d]
        la, th, bbr, bbi = _s5_discretise(a_re, a_im, log_dt, b_re, b_im)
        cr = c_re.astype(F32)
        ci = c_im.astype(F32)

        def lam_pow(e):
            m = jnp.exp(e[:, None, None] * la)
            return m * jnp.cos(e[:, None, None] * th), m * jnp.sin(e[:, None, None] * th)

        pr, pi = lam_pow(t)
        xr = pr[..., None] * bbr - pi[..., None] * bbi
        xi = pr[..., None] * bbi + pi[..., None] * bbr
        ks.append(jnp.einsum('gdn,lgnc->lgdc', cr, xr) - jnp.einsum('gdn,lgnc->lgdc', ci, xi))
        er, ei = lam_pow((S5_SUPER - 1.0 - t) if d == 0 else t)
        ins.append(jnp.stack([er[..., None] * bbr - ei[..., None] * bbi,
                              er[..., None] * bbi + ei[..., None] * bbr]))
        zr, zi = lam_pow((t + 1.0) if d == 0 else (S5_SUPER - t))
        outs.append(jnp.stack([cr[None] * zr[:, :, None, :] - ci[None] * zi[:, :, None, :],
                               -(cr[None] * zi[:, :, None, :] + ci[None] * zr[:, :, None, :])]))
        mr, mi = lam_pow(jnp.full((1,), float(S5_SUPER), F32))
        mus += [mr.reshape(4, 512), mi.reshape(4, 512)]

    def same_group(rows, per_row, cols, per_col):
        r = jnp.arange(rows)[:, None] // per_row
        c = jnp.arange(cols)[None, :] // per_col
        return r == c

    def replicate(width):
        return jnp.tile(jnp.eye(width, dtype=F32), (1, 8))

    exact = lax.Precision.HIGHEST

    lags = jnp.concatenate([ks[1][::-1], ks[0][1:]], axis=0)
    lags = lags.at[S5_SUPER - 1].add(ks[0][0])
    blk = lags.transpose(1, 3, 0, 2).reshape(4, 128, 2 * S5_SUPER - 1, S5_GROUP)
    blk = jnp.einsum('urlc,cm->ulrm', blk, replicate(S5_GROUP), precision=exact)
    w1 = jnp.where(same_group(128, S5_GROUP, 128, S5_GROUP), blk, 0.0).astype(BF16)

    def expand(v):
        sel = jnp.einsum('qp,nm->qnpm', jnp.eye(4, dtype=F32), replicate(S5_STATE))
        sel = sel.reshape(4 * S5_STATE, 4 * 8 * S5_STATE)
        out = jnp.einsum('utrk,km->utrm', v.reshape(4, S5_SUPER, 128, 4 * S5_STATE), sel,
                         precision=exact)
        mask = same_group(128, S5_GROUP, 512, S5_STATE)
        mask = jnp.tile(mask, (1, 4))
        return jnp.where(mask, out, 0.0).astype(BF16).reshape(4, 2048, 2048)

    v2 = jnp.stack(ins)
    v2 = v2.reshape(2, 2, S5_SUPER, 4, 8, S5_STATE, S5_GROUP).transpose(3, 2, 4, 6, 0, 1, 5)
    w2 = expand(v2)
    v3 = jnp.stack(outs)
    v3 = v3.reshape(2, 2, S5_SUPER, 4, 8, S5_GROUP, S5_STATE).transpose(3, 2, 4, 5, 0, 1, 6)
    w3 = expand(v3).transpose(0, 2, 1)
    return w1, w2, w3, mus


def _rw_prep_kernel(x_ref, xp_ref, xn_ref, mu_ref, aup_ref, wup0_ref, wup1_ref,
                    a0_ref, w00_ref, w01_ref, kk_ref, ka_ref, ones_ref,
                    r_o, k_o, v_o, kk_o, b_o, l0_o, l1_o, *, tile):
    i = pl.program_id(1)
    nt = pl.num_programs(1)
    x = x_ref[0].astype(F32)
    rowi = lax.broadcasted_iota(jnp.int32, x.shape, 0)
    halo = RW_HALO
    prev_row = xp_ref[0][halo - 1:halo, :].astype(F32) * jnp.where(i > 0, 1.0, 0.0)
    next_row = xn_ref[0][0:1, :].astype(F32) * jnp.where(i < nt - 1, 1.0, 0.0)
    x_prev = jnp.where(rowi == 0, prev_row, pltpu.roll(x, 1, 0))
    x_next = jnp.where(rowi == tile - 1, next_row, pltpu.roll(x, tile - 1, 0))
    nb = 0.5 * (x_prev + x_next)
    xs = x + mu_ref[...] * (nb - x)
    r = xs[:, 0:512]
    k = xs[:, 512:1024]
    v = xs[:, 1024:1536]
    lt = xs[:, 1536:1664]
    a = _sigmoid(a0_ref[...] + _dot(lt.astype(BF16), aup_ref[...]))
    tw = jnp.tanh(lt).astype(BF16)
    c = math.exp(-0.5)
    l0 = -c * _sigmoid(w00_ref[...] + _dot(tw, wup0_ref[...]))
    l1 = -c * _sigmoid(w01_ref[...] + _dot(tw, wup1_ref[...]))
    kk = k * kk_ref[...]
    s1, s2 = _split2(kk * kk)
    ss = _dot(s1, ones_ref[...]) + _dot(s2, ones_ref[...])
    kk = kk / jnp.maximum(jnp.sqrt(ss), 1e-12)
    r_o[0] = r.astype(BF16)
    k_o[0] = (k * (1.0 + (a - 1.0) * ka_ref[...])).astype(BF16)
    v_o[0] = v.astype(BF16)
    kk_o[0] = kk.astype(BF16)
    b_o[0] = (kk * a).astype(BF16)
    l0_o[0] = l0
    l1_o[0] = l1


def rw_prep(h_rw, p, tile=256):
    b, l, w = h_rw.shape
    nt = l // tile
    tb = tile // RW_HALO
    tok = lambda bi, i: (bi, i, 0)
    c2 = lambda bi, i: (0, 0)
    params = [p['mu'], p['aup'], p['wup0'], p['wup1'], p['a0'], p['w00'], p['w01'],
              p['kk'], p['ka'], p['ones']]
    in_specs = [pl.BlockSpec((1, tile, w), tok),
                pl.BlockSpec((1, RW_HALO, w), lambda bi, i: (bi, jnp.maximum(i * tb - 1, 0), 0)),
                pl.BlockSpec((1, RW_HALO, w),
                             lambda bi, i: (bi, jnp.minimum((i + 1) * tb, nt * tb - 1), 0))]
    in_specs += [pl.BlockSpec(a.shape, c2) for a in params]
    return pl.pallas_call(
        functools.partial(_rw_prep_kernel, tile=tile),
        out_shape=[jax.ShapeDtypeStruct((b, l, RW_WIDTH), d) for d in (BF16,) * 5 + (F32,) * 2],
        grid=(b, nt),
        in_specs=in_specs,
        out_specs=[pl.BlockSpec((1, tile, RW_WIDTH), tok)] * 7,
        compiler_params=pltpu.CompilerParams(dimension_semantics=("parallel", "parallel"),
                                             vmem_limit_bytes=VMEM_LIMIT),
        name="rw_prep",
    )(h_rw, h_rw, h_rw, *params)


def _rw_block_diag(x, bmask):
    xb = x.astype(BF16)
    return jnp.where(bmask, jnp.concatenate([xb, xb, xb, xb], axis=0), jnp.zeros((), BF16))


def _rw_chunk_prepare(r, k, v, kk, b, lw, masks, reverse):
    tri, strict, incl, bmask, eye = masks
    c = RW_CHUNK
    bd = functools.partial(_rw_block_diag, bmask=bmask)

    l1, l2, l3 = _split3(lw)
    cl = _dot(tri, l1) + _dot(tri, l2) + _dot(tri, l3)
    yield
    cl_end = cl[0:1, :] if reverse else cl[c - 1:c, :]
    e_neg = jnp.exp(-cl)
    e_end = jnp.exp(cl_end - cl)
    kkt = kk * jnp.exp(cl - lw)
    rt = r * jnp.exp(cl)
    lhs = jnp.concatenate([kkt, rt], axis=0).astype(BF16)
    ab = _dot_nt(lhs, bd(b * e_neg))
    ak = _dot_nt(lhs, bd(k * e_neg))
    yield
    a_bb = jnp.where(strict, ab[:c], 0.0)
    a_rb = jnp.where(incl, ab[c:], 0.0).astype(BF16)
    a_bk = jnp.where(strict, ak[:c], 0.0).astype(BF16)
    a_rk = jnp.where(incl, ak[c:], 0.0).astype(BF16)
    bdv = bd(v)

    pw = -a_bb
    t = eye + pw
    pw = _dot(pw.astype(BF16), bd(pw))
    avk = _dot(jnp.concatenate([a_bk, a_rk], axis=0), bdv)
    av = avk[:c]
    yield
    for _ in range(4):
        res = _dot(jnp.concatenate([pw, t], axis=0).astype(BF16), bd(pw))
        yield
        pw = res[:c]
        t = t + res[c:]
    inv = (t + _dot(t.astype(BF16), bd(pw))).astype(BF16)
    yield
    un = -_dot(inv, jnp.concatenate([bd(av), bd(kkt)], axis=1))
    yield
    u0 = un[:, :RW_GROUP]
    ng = un[:, RW_GROUP:].astype(BF16)
    bde = (b * e_end).astype(BF16)
    ar = _dot(a_rb, jnp.concatenate([bd(u0), bd(ng)], axis=1))
    y0 = ar[:, :RW_GROUP] + avk[c:]
    rq = (rt + ar[:, RW_GROUP:]).astype(BF16)
    trans = _dot_tn(ng, bde)
    drive = _dot_tn(jnp.concatenate([u0.astype(BF16), v.astype(BF16)], axis=0),
                    jnp.concatenate([bde, (k * e_end).astype(BF16)], axis=0))
    yield
    trans = jnp.where(bmask, trans, 0.0).astype(BF16)
    drive = jnp.where(bmask, drive, 0.0)
    return rq, y0, trans, drive, jnp.exp(cl_end)


def _rw_apply(tiles, preps, s_ref, y_ref):
    for (rows, g, lanes), (rq, y0, trans, drive, p_end) in zip(tiles, preps):
        s0 = s_ref[g]
        s0b = s0.astype(BF16)
        s_ref[g] = s0 * p_end + _dot(s0b, trans) + drive
        y_ref[rows, lanes] = y0 + _dot_nt(rq, s0b)
        if g == RW_WIDTH // RW_GROUP - 1:
            yield


def _rw_masks(reverse):
    c, g = RW_CHUNK, RW_GROUP
    r2 = lax.broadcasted_iota(jnp.int32, (c, c), 0)
    c2 = lax.broadcasted_iota(jnp.int32, (c, c), 1)
    tri = jnp.where((c2 >= r2) if reverse else (c2 <= r2), 1.0, 0.0).astype(BF16)
    t = lax.broadcasted_iota(jnp.int32, (c, g), 0)
    s = lax.broadcasted_iota(jnp.int32, (c, g), 1) & (c - 1)
    strict = (s > t) if reverse else (s < t)
    incl = (s >= t) if reverse else (s <= t)
    eye = jnp.where(s == t, 1.0, 0.0).astype(F32)
    bi = lax.broadcasted_iota(jnp.int32, (g, g), 0) >> 6
    bj = lax.broadcasted_iota(jnp.int32, (g, g), 1) >> 6
    return tri, strict, incl, bi == bj, eye


def _rw_scan_kernel(*refs, reverse, final):
    if final:
        (r_ref, k_ref, v_ref, kk_ref, b_ref, lw_ref, yprev_ref, z_ref,
         rk_ref, lng_ref, lnb_ref, ones_ref, o_ref, s_ref, y_ref) = refs
    else:
        (r_ref, k_ref, v_ref, kk_ref, b_ref, lw_ref, o_ref, s_ref, y_ref) = refs

    @pl.when(pl.program_id(1) == 0)
    def _():
        s_ref[...] = jnp.zeros_like(s_ref)

    masks = _rw_masks(reverse)
    n_chunks = RW_TILE // RW_CHUNK
    order = list(range(n_chunks - 1, -1, -1) if reverse else range(n_chunks))
    n_groups = RW_WIDTH // RW_GROUP
    pending = None
    for w in range(0, n_chunks, RW_WAVE):
        tiles = [(slice(ci * RW_CHUNK, (ci + 1) * RW_CHUNK), g, slice(g * RW_GROUP, (g + 1) * RW_GROUP))
                 for ci in order[w:w + RW_WAVE] for g in range(n_groups)]
        load = lambda ref, rows, lanes: ref[0, rows, lanes].astype(F32)
        gens = [_rw_chunk_prepare(load(r_ref, rows, lanes), load(k_ref, rows, lanes),
                                  load(v_ref, rows, lanes), load(kk_ref, rows, lanes),
                                  load(b_ref, rows, lanes), lw_ref[0, rows, lanes], masks, reverse)
                for rows, g, lanes in tiles]
        if pending is not None:
            gens.append(_rw_apply(*pending, s_ref, y_ref))
        preps = _lockstep(gens)[:len(tiles)]
        pending = (tiles, preps)
    _lockstep([_rw_apply(*pending, s_ref, y_ref)])

    if not final:
        o_ref[0] = y_ref[...].astype(BF16)
        return
    ones = ones_ref[...]
    inv_n = 1.0 / RW_HEAD

    def head_sum(x):
        return _dot(x.astype(BF16), ones)

    y = y_ref[...] + yprev_ref[0].astype(F32)
    mean = head_sum(y) * inv_n
    d = y - mean
    var = head_sum(d * d) * inv_n
    yn = d * lax.rsqrt(var + RW_LN_EPS) * lng_ref[...] + lnb_ref[...]
    bonus = head_sum(r_ref[0].astype(F32) * k_ref[0].astype(F32) * rk_ref[...]) * v_ref[0].astype(F32)
    o_ref[0] = ((yn + bonus) * _silu(z_ref[0].astype(F32))).astype(BF16)


def rw_sweep(seqs, reverse, final_args=None):
    b, l, _ = seqs[0].shape
    nt = l // RW_TILE
    final = final_args is not None
    if reverse:
        tok = lambda bi, i: (bi, nt - 1 - i, 0)
    else:
        tok = lambda bi, i: (bi, i, 0)
    c2 = lambda bi, i: (0, 0)
    tok_spec = pl.BlockSpec((1, RW_TILE, RW_WIDTH), tok)
    in_specs = [tok_spec] * 6
    args = list(seqs)
    if final:
        yprev, z, rk, lng, lnb, ones = final_args
        in_specs += [tok_spec, tok_spec]
        in_specs += [pl.BlockSpec(a.shape, c2) for a in (rk, lng, lnb, ones)]
        args += [yprev, z, rk, lng, lnb, ones]
    return pl.pallas_call(
        functools.partial(_rw_scan_kernel, reverse=reverse, final=final),
        out_shape=jax.ShapeDtypeStruct((b, l, RW_WIDTH), BF16),
        grid=(b, nt),
        in_specs=in_specs,
        out_specs=tok_spec,
        scratch_shapes=[pltpu.VMEM((RW_WIDTH // RW_GROUP, RW_GROUP, RW_GROUP), F32),
                        pltpu.VMEM((RW_TILE, RW_WIDTH), F32)],
        compiler_params=pltpu.CompilerParams(dimension_semantics=("arbitrary", "arbitrary"),
                                             vmem_limit_bytes=VMEM_LIMIT),
        name="rw_final" if final else "rw_sweep",
    )(*args)


AT_HEAD_ORDER = tuple(8 * p + 4 * par + j for p in range(2) for j in range(4) for par in range(2))
LOG2E = math.log2(math.e)


def _head_rms(t, ones_ref, expand_ref, gain_ref):
    ss = _dot((t * t).astype(BF16), ones_ref[...])
    r1, r2 = _split2(lax.rsqrt(ss * (1.0 / AT_HEAD_DIM) + EPS))
    rb = _dot(r1, expand_ref[...]) + _dot(r2, expand_ref[...])
    return (t * rb * gain_ref[...]).astype(BF16)


def _proj_in_attn_kernel(x_ref, g_ref, w_ref, oq_ref, eq_ref, gq_ref, ok_ref, ek_ref, gk_ref,
                         q_o, k_o, v_o, z_o):
    x = x_ref[...]
    ms = jnp.mean(x * x, axis=-1, keepdims=True)
    h = (x * lax.rsqrt(ms + EPS) * g_ref[...]).astype(BF16)
    c0, c1, c2 = AT_WIDTH, AT_WIDTH + AT_KV_WIDTH, AT_WIDTH + 2 * AT_KV_WIDTH
    q_o[...] = _head_rms(_dot(h, w_ref[:, :c0]), oq_ref, eq_ref, gq_ref)
    k_o[...] = _head_rms(_dot(h, w_ref[:, c0:c1]), ok_ref, ek_ref, gk_ref)
    v_o[...] = _dot(h, w_ref[:, c1:c2]).astype(BF16)
    z_o[...] = _dot(h, w_ref[:, c2:]).astype(BF16)


def proj_in_attn(x2d, p):
    m = x2d.shape[0]
    n = p['w_in'].shape[1]
    consts = [p['ones_q'], p['exp_q'], p['gain_q'], p['ones_k'], p['exp_k'], p['gain_k']]
    widths = (AT_WIDTH, AT_KV_WIDTH, AT_KV_WIDTH, AT_WIDTH)
    dtypes = (BF16, BF16, BF16, BF16)
    return pl.pallas_call(
        _proj_in_attn_kernel,
        out_shape=[jax.ShapeDtypeStruct((m, w), d) for w, d in zip(widths, dtypes)],
        grid=(m // TOK_TILE,),
        in_specs=[pl.BlockSpec((TOK_TILE, D_MODEL), lambda i: (i, 0)),
                  pl.BlockSpec((1, D_MODEL), lambda i: (0, 0)),
                  pl.BlockSpec((D_MODEL, n), lambda i: (0, 0))]
                 + [pl.BlockSpec(c.shape, lambda i: (0, 0)) for c in consts],
        out_specs=[pl.BlockSpec((TOK_TILE, w), lambda i: (i, 0)) for w in widths],
        compiler_params=pltpu.CompilerParams(dimension_semantics=("parallel",),
                                             vmem_limit_bytes=VMEM_LIMIT),
        name="proj_in_attn",
    )(x2d, p['norm'].reshape(1, D_MODEL), p['w_in'], *consts)


def _attn_kernel(sink_ref, q_ref, kp_ref, kc_ref, kn_ref, vp_ref, vc_ref, vn_ref, z_ref, bias_ref,
                 o_ref):
    blk = AT_BLOCK
    lane = lax.broadcasted_iota(jnp.int32, (3 * blk, 2 * AT_HEAD_DIM), 1)
    even_lane = lane < AT_HEAD_DIM
    zero = jnp.zeros((), BF16)

    def tile_chain(tile, k2, v2):
        cols = slice(tile * 128, (tile + 1) * 128)
        h_even, h_odd = AT_HEAD_ORDER[2 * tile], AT_HEAD_ORDER[2 * tile + 1]
        s = _dot_nt(q_ref[0][:, cols], k2)
        yield
        s = s + bias_ref[0, tile]
        se, so = s[:, :3 * blk], s[:, 3 * blk:]
        sink_e = sink_ref[h_even] * LOG2E
        sink_o = sink_ref[h_odd] * LOG2E
        me = jnp.maximum(jnp.max(se, axis=-1, keepdims=True), sink_e)
        mo = jnp.maximum(jnp.max(so, axis=-1, keepdims=True), sink_o)
        yield
        pe = jnp.exp2(se - me)
        po = jnp.exp2(so - mo)
        de = jnp.sum(pe, axis=-1, keepdims=True) + jnp.exp2(sink_e - me)
        do = jnp.sum(po, axis=-1, keepdims=True) + jnp.exp2(sink_o - mo)
        yield
        pcat = jnp.concatenate([(pe * (1.0 / de)).astype(BF16), (po * (1.0 / do)).astype(BF16)], axis=1)
        o = _dot(pcat, v2)
        yield
        o_ref[0, :, cols] = (o * _silu(z_ref[0, :, cols].astype(F32))).astype(o_ref.dtype)

    chains = []
    for p in range(2):
        lanes = slice(p * 128, (p + 1) * 128)
        kt = jnp.concatenate([kp_ref[0][:, lanes], kc_ref[0][:, lanes], kn_ref[0][:, lanes]], axis=0)
        vt = jnp.concatenate([vp_ref[0][:, lanes], vc_ref[0][:, lanes], vn_ref[0][:, lanes]], axis=0)
        k2 = jnp.concatenate([jnp.where(even_lane, kt, zero), jnp.where(even_lane, zero, kt)], axis=0)
        v2 = jnp.concatenate([jnp.where(even_lane, vt, zero), jnp.where(even_lane, zero, vt)], axis=0)
        chains += [tile_chain(4 * p + j, k2, v2) for j in range(4)]
    _lockstep(chains)


def _attn_bias():
    blk = AT_BLOCK
    iq = jnp.arange(blk)[:, None]
    jk = jnp.arange(3 * blk)[None, :]
    absrel = jnp.abs(blk + iq - jk)
    in_win = absrel <= blk
    slopes = LOG2E * jnp.exp2(-8.0 * (jnp.asarray(AT_HEAD_ORDER, F32) + 1.0) / AT_HEADS)
    alibi = -slopes[:, None, None] * absrel.astype(F32)[None]
    variants = []
    for key_lo, key_hi in ((blk, 3 * blk), (0, 3 * blk), (0, 2 * blk)):
        ok = in_win & (jk >= key_lo) & (jk < key_hi)
        b = jnp.where(ok[None], alibi, NEG_INF)
        variants.append(jnp.concatenate([b[0::2], b[1::2]], axis=-1))
    return jnp.stack(variants)


def attention(q, k, v, z, sink):
    b, l, _ = q.shape
    nb = l // AT_BLOCK
    assert nb >= 2
    cur = lambda bi, i, s: (bi, i, 0)
    prv = lambda bi, i, s: (bi, jnp.maximum(i - 1, 0), 0)
    nxt = lambda bi, i, s: (bi, jnp.minimum(i + 1, nb - 1), 0)
    edge = lambda bi, i, s: (jnp.where(i == 0, 0, jnp.where(i == nb - 1, 2, 1)), 0, 0, 0)
    kv = lambda f: pl.BlockSpec((1, AT_BLOCK, AT_KV_WIDTH), f)
    wide = pl.BlockSpec((1, AT_BLOCK, AT_WIDTH), cur)
    grid_spec = pltpu.PrefetchScalarGridSpec(
        num_scalar_prefetch=1,
        grid=(b, nb),
        in_specs=[wide, kv(prv), kv(cur), kv(nxt), kv(prv), kv(cur), kv(nxt), wide,
                  pl.BlockSpec((1, 8, AT_BLOCK, 6 * AT_BLOCK), edge)],
        out_specs=wide,
    )
    return pl.pallas_call(
        _attn_kernel,
        out_shape=jax.ShapeDtypeStruct((b, l, AT_WIDTH), BF16),
        grid_spec=grid_spec,
        compiler_params=pltpu.CompilerParams(dimension_semantics=("parallel", "parallel"),
                                             vmem_limit_bytes=VMEM_LIMIT),
        name="attn",
    )(sink, q, k, k, k, v, v, v, z, _attn_bias())


def _odd_params(j, od_norm, od_w_in, at_q_norm, at_k_norm, at_sink, od_w_out):
    cols = jnp.concatenate([jnp.arange(AT_HEAD_DIM) + h * AT_HEAD_DIM for h in AT_HEAD_ORDER])
    w = od_w_in[j]
    c0, c2 = AT_WIDTH, AT_WIDTH + 2 * AT_KV_WIDTH
    w_in = jnp.concatenate([w[:, :c0][:, cols], w[:, c0:c2], w[:, c2:][:, cols]], axis=1)

    def head_maps(width):
        head = jnp.arange(width) // AT_HEAD_DIM
        ones = (head[:, None] == jnp.arange(128)[None, :]).astype(BF16)
        return ones, ones.T

    ones_q, exp_q = head_maps(AT_WIDTH)
    ones_k, exp_k = head_maps(AT_KV_WIDTH)
    qscale = (AT_HEAD_DIM ** -0.5) * LOG2E
    return {'norm': od_norm[j], 'w_in': w_in.astype(BF16),
            'ones_q': ones_q, 'exp_q': exp_q, 'ones_k': ones_k, 'exp_k': exp_k,
            'gain_q': (jnp.tile(at_q_norm[j].astype(F32), AT_HEADS) * qscale).reshape(1, -1),
            'gain_k': jnp.tile(at_k_norm[j].astype(F32), AT_KV_HEADS).reshape(1, -1),
            'sink': at_sink[j].astype(F32), 'w_out': od_w_out[j][cols, :].astype(BF16)}


def _block_ones(width, block):
    i = jnp.arange(width) // block
    return (i[:, None] == i[None, :]).astype(BF16)


def _even_params(j, ev):
    g = lambda name: ev[name][j]
    row = lambda a: a.astype(F32).reshape(1, -1)
    p = {}
    p['norm'] = g('norm')
    p['w_in'] = g('w_in').astype(BF16)
    p['w_out_a'] = g('w_out')[:S5_WIDTH].astype(BF16)
    p['w_out_b'] = g('w_out')[S5_WIDTH:].astype(BF16)
    par = [tuple(g(n)[d] for n in ('s5_a_re', 's5_a_im', 's5_log_dt', 's5_b_re', 's5_b_im',
                                   's5_c_re', 's5_c_im')) for d in range(2)]
    p['s5_w1'], p['s5_w2'], p['s5_w3'], p['s5_mu'] = _s5_row_weights(par)
    p['s5_d'] = row(g('s5_d'))
    p['glu_w'] = g('s5_glu_w').astype(BF16)
    p['glu_b'] = row(g('s5_glu_b'))
    zeros64 = jnp.zeros((64, RW_WIDTH), F32)
    p['mu'] = row(g('rw_mu'))
    p['aup'] = jnp.concatenate([zeros64, g('rw_a_up').astype(F32)], axis=0).astype(BF16)
    p['wup0'] = jnp.concatenate([g('rw_w_up')[0].astype(F32), zeros64], axis=0).astype(BF16)
    p['wup1'] = jnp.concatenate([g('rw_w_up')[1].astype(F32), zeros64], axis=0).astype(BF16)
    p['a0'] = row(g('rw_a0'))
    p['w00'] = row(g('rw_w0')[0])
    p['w01'] = row(g('rw_w0')[1])
    p['kk'] = row(g('rw_k_k'))
    p['ka'] = row(g('rw_k_a'))
    p['ones'] = _block_ones(RW_WIDTH, RW_HEAD)
    p['rk'] = row(g('rw_r_k'))
    p['lng'] = row(g('rw_ln_g'))
    p['lnb'] = row(g('rw_ln_b'))
    return p


def _even_layer(x, p):
    b, l, _ = x.shape
    x2 = x.reshape(b * l, D_MODEL)
    u, z_s5, h_rw, z_rw = proj_in(x2, p['norm'], p['w_in'],
                                  (S5_WIDTH, S5_WIDTH, RW_SHIFTED, RW_WIDTH))
    u = u.reshape(b, l, -1)
    z_s5 = z_s5.reshape(b, l, -1)
    h_rw = h_rw.reshape(b, l, -1)
    z_rw = z_rw.reshape(b, l, -1)

    ya = s5_branch(u, z_s5, p)

    r, k, v, kk, bb, l0, l1 = rw_prep(h_rw, p)
    yb_rev = rw_sweep((r, k, v, kk, bb, l1), reverse=True)
    yb = rw_sweep((r, k, v, kk, bb, l0), reverse=False,
                  final_args=(yb_rev, z_rw, p['rk'], p['lng'], p['lnb'], p['ones']))

    out = proj_out(x2, [ya.reshape(b * l, -1), yb.reshape(b * l, -1)],
                   [p['w_out_a'], p['w_out_b']])
    return out.reshape(b, l, D_MODEL)


def _odd_layer(x, p):
    b, l, _ = x.shape
    x2 = x.reshape(b * l, D_MODEL)
    q, k, v, z = proj_in_attn(x2, p)
    sh = lambda t: t.reshape(b, l, -1)
    o = attention(sh(q), sh(k), sh(v), sh(z), p['sink'])
    out = proj_out(x2, [o.reshape(b * l, -1)], [p['w_out']])
    return out.reshape(b, l, D_MODEL)


def kernel(x_prompt, x_sample, ev_norm, ev_w_in, s5_a_re, s5_a_im, s5_log_dt, s5_b_re, s5_b_im, s5_c_re, s5_c_im, s5_d, s5_glu_w, s5_glu_b, rw_mu, rw_w0, rw_w_up, rw_a0, rw_a_up, rw_k_k, rw_k_a, rw_r_k, rw_ln_g, rw_ln_b, ev_w_out, od_norm, od_w_in, at_q_norm, at_k_norm, at_sink, od_w_out):
    ev = {'norm': ev_norm, 'w_in': ev_w_in, 's5_a_re': s5_a_re, 's5_a_im': s5_a_im,
          's5_log_dt': s5_log_dt, 's5_b_re': s5_b_re, 's5_b_im': s5_b_im,
          's5_c_re': s5_c_re, 's5_c_im': s5_c_im, 's5_d': s5_d, 's5_glu_w': s5_glu_w,
          's5_glu_b': s5_glu_b, 'rw_mu': rw_mu, 'rw_w0': rw_w0, 'rw_w_up': rw_w_up,
          'rw_a0': rw_a0, 'rw_a_up': rw_a_up, 'rw_k_k': rw_k_k, 'rw_k_a': rw_k_a,
          'rw_r_k': rw_r_k, 'rw_ln_g': rw_ln_g, 'rw_ln_b': rw_ln_b, 'w_out': ev_w_out}
    n_even = ev_norm.shape[0]
    n_odd = od_norm.shape[0]
    even = [_even_params(j, ev) for j in range(n_even)]
    odd = [_odd_params(j, od_norm, od_w_in, at_q_norm, at_k_norm, at_sink, od_w_out)
           for j in range(n_odd)]

    def trunk(x):
        for layer in range(n_even + n_odd):
            j = layer // 2
            x = _even_layer(x, even[j]) if layer % 2 == 0 else _odd_layer(x, odd[j])
        return x

    return (trunk(x_prompt), trunk(x_sample))
```
